```python
import math
import jax, jax.numpy as jnp
from jax import lax
import numpy as np

D_MODEL = 1024
BATCH = 4
SEQ = 4096
DEPTH = 2
DEC_BATCH = 32
DEC_SEQ = 8
PAST_LEN = 16384
PAGE_SIZE = 128

S5_WIDTH = D_MODEL // 2
S5_GROUP = 16
S5_GROUPS = S5_WIDTH // S5_GROUP
S5_STATE = 64
MOBA_HEADS = 8
MOBA_HEAD_DIM = 64
MOBA_WIDTH = MOBA_HEADS * MOBA_HEAD_DIM
MOBA_BLOCK = 256
MOBA_TOPK = 3
MOBA_QBLOCK = 128
RET_HEADS = 4
RET_DK = 64
RET_DV = 128
RET_QK_WIDTH = RET_HEADS * RET_DK
RET_V_WIDTH = RET_HEADS * RET_DV
RET_CHUNK = 128
D_FF = 4 * D_MODEL
ROPE_THETA = 10000.0
NORM_EPS = 1e-6
NEG_INF = -1e30
IN_SPLITS = (S5_WIDTH, MOBA_WIDTH, MOBA_WIDTH, MOBA_WIDTH, RET_QK_WIDTH, RET_QK_WIDTH,
             RET_V_WIDTH, RET_V_WIDTH, D_MODEL, D_MODEL, D_MODEL)
IN_COLS = sum(IN_SPLITS)

kernel_name = 'hybrid_s5_moba_retention_decode_step'


def rmsnorm(x, g):
    xf = x.astype(jnp.float32)
    y = xf * lax.rsqrt(jnp.mean(xf * xf, axis=-1, keepdims=True) + NORM_EPS)
    return y.astype(x.dtype) * g


def rope(x, pos):
    half = x.shape[-1] // 2
    freqs = ROPE_THETA ** (-jnp.arange(half, dtype=jnp.float32) / half)
    ang = pos.astype(jnp.float32)[:, None] * freqs[None, :]
    cos = jnp.cos(ang)[None, :, None, :]
    sin = jnp.sin(ang)[None, :, None, :]
    xf = x.astype(jnp.float32)
    x1, x2 = xf[..., :half], xf[..., half:]
    return jnp.concatenate([x1 * cos - x2 * sin, x1 * sin + x2 * cos], axis=-1).astype(x.dtype)


def s5_mixer(u, x0_re, x0_im, lam_re, lam_im, log_dt, b_re, b_im, c_re, c_im, d_skip, w_glu):
    bsz, seq_len, _ = u.shape
    f32 = jnp.float32
    lam = lax.complex(jnp.minimum(lam_re.astype(f32), -1e-4), lam_im.astype(f32))
    dt = jnp.exp(log_dt.astype(f32))[:, None]
    lam_bar = jnp.exp(lam * dt)
    b_mat = lax.complex(b_re.astype(f32), b_im.astype(f32))
    b_bar = ((lam_bar - 1.0) / lam)[..., None] * b_mat
    c_mat = lax.complex(c_re.astype(f32), c_im.astype(f32))
    uf = u.astype(f32)
    ug = uf.reshape(bsz, seq_len, S5_GROUPS, S5_GROUP).astype(jnp.complex64)
    bu = jnp.einsum('gpc,blgc->blgp', b_bar, ug)
    x0 = lax.complex(x0_re.astype(f32), x0_im.astype(f32))
    bu = bu.at[:, 0].add(lam_bar * x0)
    a = jnp.broadcast_to(lam_bar, bu.shape)

    def combine(e1, e2):
        a1, b1 = e1
        a2, b2 = e2
        return a1 * a2, a2 * b1 + b2

    _, xs = lax.associative_scan(combine, (a, bu), axis=1)
    y = jnp.einsum('gcp,blgp->blgc', c_mat, xs).real.reshape(bsz, seq_len, S5_WIDTH)
    y = y + d_skip.astype(f32) * uf
    y = jax.nn.gelu(y)
    y = y * jax.nn.sigmoid(y @ w_glu.astype(f32))
    x_last = xs[:, -1]
    return y.astype(u.dtype), x_last.real, x_last.imag


def moba_query(q, q_pos, kb, vb, kmean):
    bsz, n_heads, n_blocks, blk, dh = kb.shape
    n_q = q.shape[1]
    scale = dh ** -0.5
    own = q_pos // MOBA_BLOCK
    gate = jnp.einsum('bqhd,bhnd->bhqn', q.astype(jnp.float32), kmean)
    past = jnp.arange(n_blocks)[None, :] < own[:, None]
    gate = jnp.where(past[None, None], gate, NEG_INF)
    topk = min(MOBA_TOPK, n_blocks)
    _, idx = lax.top_k(gate, topk)
    sel_ok = idx < own[None, None, :, None]
    bi = jnp.arange(bsz)[:, None, None]
    hi = jnp.arange(n_heads)[None, :, None]
    ks = kb[bi[..., None], hi[..., None], idx]
    vs = vb[bi[..., None], hi[..., None], idx]
    own_b = own[None, None, :]
    ko = kb[bi, hi, own_b]
    vo = vb[bi, hi, own_b]
    s_sel = jnp.einsum('bqhd,bhqkjd->bhqkj', q, ks).astype(jnp.float32) * scale
    s_sel = jnp.where(sel_ok[..., None], s_sel, NEG_INF).reshape(bsz, n_heads, n_q, topk * blk)
    s_own = jnp.einsum('bqhd,bhqjd->bhqj', q, ko).astype(jnp.float32) * scale
    key_pos = own[:, None] * MOBA_BLOCK + jnp.arange(blk)[None, :]
    s_own = jnp.where((key_pos <= q_pos[:, None])[None, None], s_own, NEG_INF)
    probs = jax.nn.softmax(jnp.concatenate([s_sel, s_own], axis=-1), axis=-1)
    p_sel = probs[..., :topk * blk].reshape(bsz, n_heads, n_q, topk, blk).astype(q.dtype)
    p_own = probs[..., topk * blk:].astype(q.dtype)
    out = jnp.einsum('bhqkj,bhqkjd->bqhd', p_sel, vs) + jnp.einsum('bhqj,bhqjd->bqhd', p_own, vo)
    return out


def moba_attention(q, q_pos, k_all, v_all):
    bsz, lk, n_heads, dh = k_all.shape
    pad = (-lk) % MOBA_BLOCK
    k_all = jnp.pad(k_all, ((0, 0), (0, pad), (0, 0), (0, 0)))
    v_all = jnp.pad(v_all, ((0, 0), (0, pad), (0, 0), (0, 0)))
    n_blocks = (lk + pad) // MOBA_BLOCK
    kb = k_all.reshape(bsz, n_blocks, MOBA_BLOCK, n_heads, dh).transpose(0, 3, 1, 2, 4)
    vb = v_all.reshape(bsz, n_blocks, MOBA_BLOCK, n_heads, dh).transpose(0, 3, 1, 2, 4)
    kmean = jnp.mean(kb.astype(jnp.float32), axis=3)
    lq = q.shape[1]
    qb = min(MOBA_QBLOCK, lq)
    n_qb = lq // qb
    q_blocks = q.reshape(bsz, n_qb, qb, n_heads, dh).transpose(1, 0, 2, 3, 4)
    p_blocks = q_pos.reshape(n_qb, qb)
    out = lax.map(lambda a: moba_query(a[0], a[1], kb, vb, kmean), (q_blocks, p_blocks))
    return out.transpose(1, 0, 2, 3, 4).reshape(bsz, lq, n_heads, dh)


def retention(q, k, v, s0, chunk):
    bsz, seq_len, n_heads, dk = q.shape
    dv = v.shape[-1]
    f32 = jnp.float32
    log_g = jnp.log(1.0 - 2.0 ** (-5.0 - jnp.arange(n_heads, dtype=f32)))
    n_chunks = seq_len // chunk

    def to_chunks(t, d):
        return t.astype(f32).reshape(bsz, n_chunks, chunk, n_heads, d).transpose(1, 0, 3, 2, 4)

    qc, kc, vc = to_chunks(q, dk), to_chunks(k, dk), to_chunks(v, dv)
    idx = jnp.arange(chunk, dtype=f32)
    diff = idx[:, None] - idx[None, :]
    decay_mask = jnp.where(diff >= 0, jnp.exp(log_g[:, None, None] * jnp.maximum(diff, 0.0)), 0.0)
    q_decay = jnp.exp(log_g[:, None] * (idx + 1.0))[None, :, :, None]
    k_decay = jnp.exp(log_g[:, None] * (chunk - 1.0 - idx))[None, :, :, None]
    chunk_decay = jnp.exp(log_g * chunk)[None, :, None, None]

    def step(s, inp):
        qi, ki, vi = inp
        inner = jnp.einsum('bhid,bhjd->bhij', qi, ki) * decay_mask
        o = jnp.einsum('bhij,bhjv->bhiv', inner, vi) + jnp.einsum('bhid,bhdv->bhiv', qi, s) * q_decay
        s = s * chunk_decay + jnp.einsum('bhjd,bhjv->bhdv', ki * k_decay, vi)
        return s, o

    s_final, o = lax.scan(step, s0.astype(f32), (qc, kc, vc))
    o = o.transpose(1, 0, 3, 2, 4).reshape(bsz, seq_len, n_heads, dv)
    return o, s_final


def trunk_layer(x, pos, k_past, v_past, s5_re0, s5_im0, ret_s0, p):
    bsz, seq_len, _ = x.shape
    h = rmsnorm(x, p['norm1_g'])
    proj = h @ p['w_in']
    offsets = [int(o) for o in np.cumsum(IN_SPLITS)[:-1]]
    (u_s5, q_m, k_m, v_m, q_r, k_r, v_r, g_r, gate_s5, gate_moba, gate_ret) = jnp.split(proj, offsets, axis=-1)

    y_s5, s5_re, s5_im = s5_mixer(u_s5, s5_re0, s5_im0, p['s5_lambda_re'], p['s5_lambda_im'], p['s5_log_dt'],
                                  p['s5_b_re'], p['s5_b_im'], p['s5_c_re'], p['s5_c_im'], p['s5_d'], p['s5_w_glu'])

    q_m = rope(q_m.reshape(bsz, seq_len, MOBA_HEADS, MOBA_HEAD_DIM), pos)
    k_m = rope(k_m.reshape(bsz, seq_len, MOBA_HEADS, MOBA_HEAD_DIM), pos)
    v_m = v_m.reshape(bsz, seq_len, MOBA_HEADS, MOBA_HEAD_DIM)
    if k_past is None:
        k_all, v_all = k_m, v_m
    else:
        k_all = jnp.concatenate([k_past.astype(k_m.dtype), k_m], axis=1)
        v_all = jnp.concatenate([v_past.astype(v_m.dtype), v_m], axis=1)
    y_moba = moba_attention(q_m, pos, k_all, v_all).reshape(bsz, seq_len, MOBA_WIDTH)

    q_r = rope(q_r.reshape(bsz, seq_len, RET_HEADS, RET_DK), pos)
    k_r = rope(k_r.reshape(bsz, seq_len, RET_HEADS, RET_DK), pos) * (RET_DK ** -0.5)
    v_r = v_r.reshape(bsz, seq_len, RET_HEADS, RET_DV)
    chunk = RET_CHUNK if seq_len % RET_CHUNK == 0 else seq_len
    o_r, ret_s = retention(q_r, k_r, v_r, ret_s0, chunk)
    mu = jnp.mean(o_r, axis=-1, keepdims=True)
    var = jnp.mean(jnp.square(o_r - mu), axis=-1, keepdims=True)
    o_r = ((o_r - mu) * lax.rsqrt(var + NORM_EPS)).reshape(bsz, seq_len, RET_V_WIDTH).astype(x.dtype)
    y_ret = jax.nn.silu(g_r) * o_r

    merged = (jax.nn.sigmoid(gate_s5) * (y_s5 @ p['w_br_s5'])
              + jax.nn.sigmoid(gate_moba) * (y_moba @ p['w_br_moba'])
              + jax.nn.sigmoid(gate_ret) * (y_ret @ p['w_br_ret']))
    x = x + merged @ p['w_out']

    h2 = rmsnorm(x, p['norm2_g'])
    x = x + jnp.square(jax.nn.relu(h2 @ p['w_mlp_up'])) @ p['w_mlp_down']
    return x, k_m, v_m, s5_re, s5_im, ret_s


def setup_inputs(seed: int = 0) -> dict:
    key = jax.random.key(seed)
    ks = jax.random.split(key, 32)
    f32 = jnp.float32
    n_pages = PAST_LEN // PAGE_SIZE
    n_used = DEC_BATCH * n_pages
    n_pool = n_used + n_used // 4

    def nrm(k, shape, scale):
        return jax.random.normal(k, shape, f32) * scale

    x_prompt = nrm(ks[0], (BATCH, SEQ, D_MODEL), 1.0)
    x_sample = nrm(ks[1], (DEC_BATCH, DEC_SEQ, D_MODEL), 1.0)
    cache_k = nrm(ks[2], (DEPTH, n_pool, PAGE_SIZE, MOBA_HEADS, MOBA_HEAD_DIM), 1.0)
    cache_v = nrm(ks[3], (DEPTH, n_pool, PAGE_SIZE, MOBA_HEADS, MOBA_HEAD_DIM), 1.0)
    state_s5_re = nrm(ks[4], (DEPTH, DEC_BATCH, S5_GROUPS, S5_STATE), 0.5)
    state_s5_im = nrm(ks[5], (DEPTH, DEC_BATCH, S5_GROUPS, S5_STATE), 0.5)
    state_ret = nrm(ks[6], (DEPTH, DEC_BATCH, RET_HEADS, RET_DK, RET_DV), 0.5)
    page_table = jax.random.permutation(ks[7], n_pool)[:n_used].reshape(DEC_BATCH, n_pages).astype(jnp.int32)
    norm1_g = 1.0 + nrm(ks[8], (DEPTH, D_MODEL), 0.02)
    w_in = nrm(ks[9], (DEPTH, D_MODEL, IN_COLS), D_MODEL ** -0.5)
    s5_lambda_re = -0.5 + nrm(ks[10], (DEPTH, S5_GROUPS, S5_STATE), 0.01)
    s5_lambda_im = math.pi * jnp.arange(S5_STATE, dtype=f32) + nrm(ks[11], (DEPTH, S5_GROUPS, S5_STATE), 0.01)
    s5_log_dt = jax.random.uniform(ks[12], (DEPTH, S5_GROUPS), f32, math.log(1e-3), math.log(1e-1))
    s5_b_re = nrm(ks[13], (DEPTH, S5_GROUPS, S5_STATE, S5_GROUP), (2 * S5_GROUP) ** -0.5)
    s5_b_im = nrm(ks[14], (DEPTH, S5_GROUPS, S5_STATE, S5_GROUP), (2 * S5_GROUP) ** -0.5)
    s5_c_re = nrm(ks[15], (DEPTH, S5_GROUPS, S5_GROUP, S5_STATE), (2 * S5_STATE) ** -0.5)
    s5_c_im = nrm(ks[16], (DEPTH, S5_GROUPS, S5_GROUP, S5_STATE), (2 * S5_STATE) ** -0.5)
    s5_d = nrm(ks[17], (DEPTH, S5_WIDTH), 1.0)
    s5_w_glu = nrm(ks[18], (DEPTH, S5_WIDTH, S5_WIDTH), S5_WIDTH ** -0.5)
    w_br_s5 = nrm(ks[19], (DEPTH, S5_WIDTH, D_MODEL), S5_WIDTH ** -0.5)
    w_br_moba = nrm(ks[20], (DEPTH, MOBA_WIDTH, D_MODEL), MOBA_WIDTH ** -0.5)
    w_br_ret = nrm(ks[21], (DEPTH, RET_V_WIDTH, D_MODEL), RET_V_WIDTH ** -0.5)
    w_out = nrm(ks[22], (DEPTH, D_MODEL, D_MODEL), D_MODEL ** -0.5)
    norm2_g = 1.0 + nrm(ks[23], (DEPTH, D_MODEL), 0.02)
    w_mlp_up = nrm(ks[24], (DEPTH, D_MODEL, D_FF), D_MODEL ** -0.5)
    w_mlp_down = nrm(ks[25], (DEPTH, D_FF, D_MODEL), D_FF ** -0.5)
    final_g = 1.0 + nrm(ks[26], (D_MODEL,), 0.02)
    return {'x_prompt': x_prompt, 'x_sample': x_sample, 'cache_k': cache_k, 'cache_v': cache_v,
            'state_s5_re': state_s5_re, 'state_s5_im': state_s5_im, 'state_ret': state_ret,
            'page_table': page_table, 'norm1_g': norm1_g, 'w_in': w_in,
            's5_lambda_re': s5_lambda_re, 's5_lambda_im': s5_lambda_im, 's5_log_dt': s5_log_dt,
            's5_b_re': s5_b_re, 's5_b_im': s5_b_im, 's5_c_re': s5_c_re, 's5_c_im': s5_c_im,
            's5_d': s5_d, 's5_w_glu': s5_w_glu, 'w_br_s5': w_br_s5, 'w_br_moba': w_br_moba,
            'w_br_ret': w_br_ret, 'w_out': w_out, 'norm2_g': norm2_g, 'w_mlp_up': w_mlp_up,
            'w_mlp_down': w_mlp_down, 'final_g': final_g}


def reference(x_prompt, x_sample, cache_k, cache_v, state_s5_re, state_s5_im, state_ret, page_table,
              norm1_g, w_in, s5_lambda_re, s5_lambda_im, s5_log_dt, s5_b_re, s5_b_im, s5_c_re, s5_c_im,
              s5_d, s5_w_glu, w_br_s5, w_br_moba, w_br_ret, w_out, norm2_g, w_mlp_up, w_mlp_down, final_g):
    bsz_p, seq_p, _ = x_prompt.shape
    bsz_s, seq_s, _ = x_sample.shape
    n_pages = page_table.shape[1]
    past_len = n_pages * PAGE_SIZE
    pos_p = jnp.arange(seq_p, dtype=jnp.int32)
    pos_s = past_len + jnp.arange(seq_s, dtype=jnp.int32)
    zeros_s5 = jnp.zeros((bsz_p, S5_GROUPS, S5_STATE), jnp.float32)
    zeros_ret = jnp.zeros((bsz_p, RET_HEADS, RET_DK, RET_DV), jnp.float32)

    hp, hs = x_prompt, x_sample
    kp_l, vp_l, ks_l, vs_l = [], [], [], []
    s5rp_l, s5ip_l, s5rs_l, s5is_l, retp_l, rets_l = [], [], [], [], [], []
    for l in range(DEPTH):
        p = {'norm1_g': norm1_g[l], 'w_in': w_in[l], 's5_lambda_re': s5_lambda_re[l],
             's5_lambda_im': s5_lambda_im[l], 's5_log_dt': s5_log_dt[l], 's5_b_re': s5_b_re[l],
             's5_b_im': s5_b_im[l], 's5_c_re': s5_c_re[l], 's5_c_im': s5_c_im[l], 's5_d': s5_d[l],
             's5_w_glu': s5_w_glu[l], 'w_br_s5': w_br_s5[l], 'w_br_moba': w_br_moba[l],
             'w_br_ret': w_br_ret[l], 'w_out': w_out[l], 'norm2_g': norm2_g[l],
             'w_mlp_up': w_mlp_up[l], 'w_mlp_down': w_mlp_down[l]}
        hp, kp, vp, s5rp, s5ip, retp = trunk_layer(hp, pos_p, None, None, zeros_s5, zeros_s5, zeros_ret, p)
        k_past = cache_k[l][page_table].reshape(bsz_s, past_len, MOBA_HEADS, MOBA_HEAD_DIM)
        v_past = cache_v[l][page_table].reshape(bsz_s, past_len, MOBA_HEADS, MOBA_HEAD_DIM)
        hs, ks_, vs_, s5rs, s5is, rets = trunk_layer(hs, pos_s, k_past, v_past, state_s5_re[l], state_s5_im[l],
                                                      state_ret[l], p)
        kp_l.append(kp); vp_l.append(vp); ks_l.append(ks_); vs_l.append(vs_)
        s5rp_l.append(s5rp); s5ip_l.append(s5ip); s5rs_l.append(s5rs); s5is_l.append(s5is)
        retp_l.append(retp); rets_l.append(rets)

    y_prompt = rmsnorm(hp, final_g)
    y_sample = rmsnorm(hs, final_g)
    k_prompt = jnp.stack(kp_l)
    v_prompt = jnp.stack(vp_l)
    k_sample = jnp.stack(ks_l)
    v_sample = jnp.stack(vs_l)
    s5_re_prompt = jnp.stack(s5rp_l)
    s5_im_prompt = jnp.stack(s5ip_l)
    s5_re_sample = jnp.stack(s5rs_l)
    s5_im_sample = jnp.stack(s5is_l)
    ret_prompt = jnp.stack(retp_l)
    ret_sample = jnp.stack(rets_l)
    return (y_prompt, y_sample, k_prompt, v_prompt, k_sample, v_sample, s5_re_prompt, s5_im_prompt,
            s5_re_sample, s5_im_sample, ret_prompt, ret_sample)
```

```python
import functools
import math

import jax
import jax.numpy as jnp
from jax import lax
from jax.experimental import pallas as pl
from jax.experimental.pallas import tpu as pltpu

F32 = jnp.float32
BF16 = jnp.bfloat16

D_MODEL = 1024
PAGE_SIZE = 128
S5_WIDTH = 512
S5_GROUP = 16
S5_GROUPS = 32
S5_STATE = 64
S5_FLAT = S5_GROUPS * S5_STATE
MOBA_HEADS = 8
MOBA_HEAD_DIM = 64
MOBA_WIDTH = 512
MOBA_BLOCK = 256
MOBA_TOPK = 3
MOBA_QBLOCK = 128
RET_HEADS = 4
RET_DK = 64
RET_DV = 128
RET_QK_WIDTH = 256
RET_V_WIDTH = 512
RET_CHUNK = 128
D_FF = 4096
ROPE_THETA = 10000.0
NORM_EPS = 1e-6
NEG_INF = -1e30
IN_SPLITS = (S5_WIDTH, MOBA_WIDTH, MOBA_WIDTH, MOBA_WIDTH, RET_QK_WIDTH, RET_QK_WIDTH,
             RET_V_WIDTH, RET_V_WIDTH, D_MODEL, D_MODEL, D_MODEL)
IN_COLS = sum(IN_SPLITS)
(OFF_U, OFF_QM, OFF_KM, OFF_VM, OFF_QR, OFF_KR, OFF_VR, OFF_GR, OFF_GATES) = (
    0, 512, 1024, 1536, 2048, 2304, 2560, 3072, 3584)

LANES = 128
SUBLANES = 8
ROW_TILE = 256
S5_BUNDLE = 8
S5_LANE_GROUP = 512
SAMPLE_PAGES_PER_STEP = 8
VMEM_LIMIT = 56 * 1024 * 1024

NT_DIMS = (((1,), (1,)), ((), ()))
TN_DIMS = (((0,), (0,)), ((), ()))


def _const_spec(shape):
    nd = len(shape)
    return pl.BlockSpec(shape, lambda *_: (0,) * nd, pipeline_mode=pl.Buffered(1))


def _params(n_axes):
    return pltpu.CompilerParams(dimension_semantics=("arbitrary",) * n_axes,
                                vmem_limit_bytes=VMEM_LIMIT)


def _rmsnorm(x, g):
    return x * lax.rsqrt(jnp.mean(x * x, axis=-1, keepdims=True) + NORM_EPS) * g


def _mm(a, w):
    return jnp.dot(a.astype(BF16), w, preferred_element_type=F32)


def _rope(t, cos, sin_signed):
    n = t.shape[1]
    reps = n // LANES
    c = jnp.concatenate([cos] * reps, axis=1)
    s = jnp.concatenate([sin_signed] * reps, axis=1)
    lane = lax.broadcasted_iota(jnp.int32, t.shape, 1)
    first_half = (lane % MOBA_HEAD_DIM) < (MOBA_HEAD_DIM // 2)
    partner = jnp.where(first_half, pltpu.roll(t, n - MOBA_HEAD_DIM // 2, 1),
                        pltpu.roll(t, MOBA_HEAD_DIM // 2, 1))
    return t * c + partner * s


def _inproj_body(x_ref, g_ref, w_ref, cos_ref, sin_ref,
                 u_ref, qm_ref, km_ref, vm_ref, kb_ref, vb_ref, qr_ref, kr_ref, vr_ref, gr_ref,
                 gate_ref, ksum_ref):
    hb = _rmsnorm(x_ref[...], g_ref[...]).astype(BF16)
    cos = cos_ref[...]
    sin = sin_ref[...]

    def proj(lo, n):
        return jnp.dot(hb, w_ref[:, lo:lo + n], preferred_element_type=F32)

    u_ref[...] = proj(OFF_U, S5_WIDTH)
    qm_ref[...] = _rope(proj(OFF_QM, MOBA_WIDTH), cos, sin)
    km = _rope(proj(OFF_KM, MOBA_WIDTH), cos, sin)
    km_ref[...] = km
    kb_ref[...] = km.astype(BF16)
    ksum_ref[0] = jnp.sum(km, axis=0, keepdims=True)
    vm = proj(OFF_VM, MOBA_WIDTH)
    vm_ref[...] = vm
    vb_ref[...] = vm.astype(BF16)
    qr_ref[...] = _rope(proj(OFF_QR, RET_QK_WIDTH), cos, sin)
    kr_ref[...] = _rope(proj(OFF_KR, RET_QK_WIDTH), cos, sin) * (RET_DK ** -0.5)
    vr_ref[...] = proj(OFF_VR, RET_V_WIDTH)
    gr = proj(OFF_GR, RET_V_WIDTH)
    gr_ref[...] = gr * jax.nn.sigmoid(gr)
    for i in range(3):
        gate_ref[:, i * D_MODEL:(i + 1) * D_MODEL] = jax.nn.sigmoid(proj(OFF_GATES + i * D_MODEL, D_MODEL))


def _inproj(x, g, w, cos, sin):
    n = x.shape[0]
    nt = n // ROW_TILE
    row = lambda width: pl.BlockSpec((ROW_TILE, width), lambda i: (i, 0))
    widths = (S5_WIDTH, MOBA_WIDTH, MOBA_WIDTH, MOBA_WIDTH, MOBA_WIDTH, MOBA_WIDTH,
              RET_QK_WIDTH, RET_QK_WIDTH, RET_V_WIDTH, RET_V_WIDTH, 3 * D_MODEL)
    dtypes = (F32, F32, F32, F32, BF16, BF16, F32, F32, F32, F32, F32)
    out_shape = [jax.ShapeDtypeStruct((n, wd), dt) for wd, dt in zip(widths, dtypes)]
    out_shape.append(jax.ShapeDtypeStruct((nt, 1, MOBA_WIDTH), F32))
    out_specs = [row(wd) for wd in widths]
    out_specs.append(pl.BlockSpec((1, 1, MOBA_WIDTH), lambda i: (i, 0, 0)))
    return pl.pallas_call(
        _inproj_body,
        out_shape=out_shape,
        grid=(nt,),
        in_specs=[row(D_MODEL), _const_spec((1, D_MODEL)), _const_spec((D_MODEL, IN_COLS)),
                  row(LANES), row(LANES)],
        out_specs=out_specs,
        compiler_params=_params(1),
        name="inproj",
    )(x, g, w, cos, sin)


def _gelu_tanh(x):
    cdf = 0.5 * (1.0 + jnp.tanh(math.sqrt(2.0 / math.pi) * (x + 0.044715 * (x * x * x))))
    return x * cdf


def _s5_body(u_ref, x0r_ref, x0i_ref, lamr_ref, lami_ref, ljr_ref, lji_ref, wbr_ref, wbi_ref,
             wcr_ref, wci_ref, d_ref, wg_ref,
             y_ref, xlr_ref, xli_ref,
             xr_s, xi_s, cr_s, ci_s, carr_s, cari_s, *, steps, chained):
    n_bundles = S5_GROUPS // S5_BUNDLE
    in_w = S5_BUNDLE * S5_GROUP
    st_w = S5_BUNDLE * S5_STATE
    tiles_per_bundle = st_w // LANES
    tiles_per_group = S5_LANE_GROUP // LANES
    u = u_ref[...]
    ub = u.astype(BF16)
    for j in range(n_bundles):
        uj = ub[:, j * in_w:(j + 1) * in_w]
        bu_r = jnp.dot(uj, wbr_ref[j], preferred_element_type=F32)
        bu_i = jnp.dot(uj, wbi_ref[j], preferred_element_type=F32)
        for c in range(tiles_per_bundle):
            xr_s[j * tiles_per_bundle + c] = bu_r[:, c * LANES:(c + 1) * LANES]
            xi_s[j * tiles_per_bundle + c] = bu_i[:, c * LANES:(c + 1) * LANES]

    def scan(init_r, init_i, lo, store):
        lanes = slice(lo, lo + S5_LANE_GROUP)
        lr = lamr_ref[:, lanes]
        li = lami_ref[:, lanes]
        tile0 = lo // LANES

        def step(t, carry):
            xr, xi = carry
            rows = pl.ds(t, SUBLANES, stride=steps)
            bu_r = jnp.concatenate([xr_s[tile0 + c, rows, :] for c in range(tiles_per_group)], axis=1)
            bu_i = jnp.concatenate([xi_s[tile0 + c, rows, :] for c in range(tiles_per_group)], axis=1)
            nr = lr * xr - li * xi + bu_r
            ni = lr * xi + li * xr + bu_i
            if store:
                for c in range(tiles_per_group):
                    xr_s[tile0 + c, rows, :] = nr[:, c * LANES:(c + 1) * LANES]
                    xi_s[tile0 + c, rows, :] = ni[:, c * LANES:(c + 1) * LANES]
            return nr, ni

        return lax.fori_loop(0, steps, step, (init_r, init_i))

    if chained:
        @pl.when(pl.program_id(1) == 0)
        def _():
            carr_s[...] = x0r_ref[...]
            cari_s[...] = x0i_ref[...]

    zeros = jnp.zeros((SUBLANES, S5_LANE_GROUP), F32)
    for lo in range(0, S5_FLAT, S5_LANE_GROUP):
        lanes = slice(lo, lo + S5_LANE_GROUP)
        if chained:
            end_r, end_i = scan(zeros, zeros, lo, store=False)
            c_r = carr_s[:, lanes]
            c_i = cari_s[:, lanes]
            ljr = ljr_ref[:, lanes]
            lji = lji_ref[:, lanes]
            for s in range(SUBLANES):
                cr_s[s:s + 1, lanes] = c_r
                ci_s[s:s + 1, lanes] = c_i
                n_r = ljr * c_r - lji * c_i + end_r[s:s + 1, :]
                n_i = ljr * c_i + lji * c_r + end_i[s:s + 1, :]
                c_r, c_i = n_r, n_i
            carr_s[:, lanes] = c_r
            cari_s[:, lanes] = c_i
            scan(cr_s[:, lanes], ci_s[:, lanes], lo, store=True)
        else:
            f_r, f_i = scan(x0r_ref[:, lanes], x0i_ref[:, lanes], lo, store=True)
            xlr_ref[:, lanes] = f_r
            xli_ref[:, lanes] = f_i
    if chained:
        xlr_ref[...] = carr_s[...]
        xli_ref[...] = cari_s[...]

    ys = []
    for j in range(n_bundles):
        tiles = range(j * tiles_per_bundle, (j + 1) * tiles_per_bundle)
        x_r = jnp.concatenate([xr_s[c] for c in tiles], axis=1)
        x_i = jnp.concatenate([xi_s[c] for c in tiles], axis=1)
        ys.append(_mm(x_r, wcr_ref[j]) + _mm(x_i, wci_ref[j]))
    y = jnp.concatenate(ys, axis=1) + d_ref[...] * u
    y = _gelu_tanh(y)
    y_ref[...] = y * jax.nn.sigmoid(_mm(y, wg_ref[...]))


def _s5_mixer(u, x0r, x0i, prm, *, steps, chained):
    n_groups, rows, _ = u.shape
    t = SUBLANES * steps
    n_chunks = rows // t
    r0 = x0r.shape[1]
    chunk = pl.BlockSpec((None, t, S5_WIDTH), lambda b, c: (b, c, 0))
    state = pl.BlockSpec((None, r0, S5_FLAT), lambda b, c: (b, 0, 0))
    n_bundles = S5_GROUPS // S5_BUNDLE
    in_w = S5_BUNDLE * S5_GROUP
    st_w = S5_BUNDLE * S5_STATE
    body = functools.partial(_s5_body, steps=steps, chained=chained)
    return pl.pallas_call(
        body,
        out_shape=[jax.ShapeDtypeStruct(u.shape, F32),
                   jax.ShapeDtypeStruct(x0r.shape, F32), jax.ShapeDtypeStruct(x0r.shape, F32)],
        grid=(n_groups, n_chunks),
        in_specs=[chunk, state, state,
                  _const_spec((SUBLANES, S5_FLAT)), _const_spec((SUBLANES, S5_FLAT)),
                  _const_spec((1, S5_FLAT)), _const_spec((1, S5_FLAT)),
                  _const_spec((n_bundles, in_w, st_w)), _const_spec((n_bundles, in_w, st_w)),
                  _const_spec((n_bundles, st_w, in_w)), _const_spec((n_bundles, st_w, in_w)),
                  _const_spec((1, S5_WIDTH)), _const_spec((S5_WIDTH, S5_WIDTH))],
        out_specs=[chunk, state, state],
        scratch_shapes=[pltpu.VMEM((S5_FLAT // LANES, t, LANES), F32),
                        pltpu.VMEM((S5_FLAT // LANES, t, LANES), F32),
                        pltpu.VMEM((SUBLANES, S5_FLAT), F32), pltpu.VMEM((SUBLANES, S5_FLAT), F32),
                        pltpu.VMEM((1, S5_FLAT), F32), pltpu.VMEM((1, S5_FLAT), F32)],
        compiler_params=_params(2),
        name="s5_chained" if chained else "s5_independent",
    )(u, x0r, x0i, prm["lam_r8"], prm["lam_i8"], prm["lj_r"][steps], prm["lj_i"][steps],
      prm["wb_r"], prm["wb_i"], prm["wc_r"], prm["wc_i"], prm["d"], prm["w_glu"])


def _s5_prepare(lam_re, lam_im, log_dt, b_re, b_im, c_re, c_im, d_skip, w_glu, step_counts):
    lam = lax.complex(jnp.minimum(lam_re.astype(F32), -1e-4), lam_im.astype(F32))
    dt = jnp.exp(log_dt.astype(F32))[:, None]
    lam_bar = jnp.exp(lam * dt)
    b_bar = ((lam_bar - 1.0) / lam)[..., None] * lax.complex(b_re.astype(F32), b_im.astype(F32))
    n_bundles = S5_GROUPS // S5_BUNDLE
    eye = jnp.eye(S5_BUNDLE, dtype=F32)

    def expand_b(t):
        t = t.reshape(n_bundles, S5_BUNDLE, S5_STATE, S5_GROUP)
        w = jnp.einsum("jgpc,gh->jgchp", t, eye)
        return w.reshape(n_bundles, S5_BUNDLE * S5_GROUP, S5_BUNDLE * S5_STATE).astype(BF16)

    def expand_c(t):
        t = t.reshape(n_bundles, S5_BUNDLE, S5_GROUP, S5_STATE)
        w = jnp.einsum("jgcp,gh->jgphc", t, eye)
        return w.reshape(n_bundles, S5_BUNDLE * S5_STATE, S5_BUNDLE * S5_GROUP).astype(BF16)

    flat = lambda t: t.reshape(1, S5_FLAT)
    lam_r = flat(jnp.real(lam_bar))
    lam_i = flat(jnp.imag(lam_bar))
    lj_r, lj_i = {}, {}
    for steps in step_counts:
        lam_pow = jnp.exp(lam * dt * float(steps))
        lj_r[steps] = flat(jnp.real(lam_pow))
        lj_i[steps] = flat(jnp.imag(lam_pow))
    return {
        "lam_r8": jnp.broadcast_to(lam_r, (SUBLANES, S5_FLAT)),
        "lam_i8": jnp.broadcast_to(lam_i, (SUBLANES, S5_FLAT)),
        "lj_r": lj_r, "lj_i": lj_i,
        "wb_r": expand_b(jnp.real(b_bar)), "wb_i": expand_b(jnp.imag(b_bar)),
        "wc_r": expand_c(c_re.astype(F32)), "wc_i": expand_c(-c_im.astype(F32)),
        "d": d_skip.astype(F32).reshape(1, S5_WIDTH),
        "w_glu": w_glu.astype(BF16),
    }


def _retention_body(q_ref, k_ref, v_ref, g_ref, s0_ref, dm_ref, qd_ref, kd_ref, cd_ref,
                    y_ref, sout_ref, s_s):
    @pl.when(pl.program_id(1) == 0)
    def _():
        s_s[...] = s0_ref[...]

    for h in range(RET_HEADS):
        qk = slice(h * RET_DK, (h + 1) * RET_DK)
        vv = slice(h * RET_DV, (h + 1) * RET_DV)
        q = q_ref[:, qk].astype(BF16)
        k = k_ref[:, qk]
        v = v_ref[:, vv].astype(BF16)
        s = s_s[h]
        inner = lax.dot_general(q, k.astype(BF16), NT_DIMS, preferred_element_type=F32) * dm_ref[h]
        o = (jnp.dot(inner.astype(BF16), v, preferred_element_type=F32)
             + jnp.dot(q, s.astype(BF16), preferred_element_type=F32) * qd_ref[h])
        k_dec = (k * kd_ref[h]).astype(BF16)
        s_s[h] = s * cd_ref[h] + lax.dot_general(k_dec, v, TN_DIMS, preferred_element_type=F32)
        mu = jnp.mean(o, axis=-1, keepdims=True)
        var = jnp.mean(jnp.square(o - mu), axis=-1, keepdims=True)
        y_ref[:, vv] = g_ref[:, vv] * ((o - mu) * lax.rsqrt(var + NORM_EPS))
    sout_ref[...] = s_s[...]


def _retention(q, k, v, g, s0, chunk):
    bsz, seq_len, _ = q.shape
    n_chunks = seq_len // chunk
    log_g = jnp.log(1.0 - 2.0 ** (-5.0 - jnp.arange(RET_HEADS, dtype=F32)))
    idx = jnp.arange(chunk, dtype=F32)
    diff = idx[:, None] - idx[None, :]
    decay_mask = jnp.where(diff >= 0, jnp.exp(log_g[:, None, None] * jnp.maximum(diff, 0.0)), 0.0)
    q_decay = jnp.exp(log_g[:, None] * (idx + 1.0))[:, :, None]
    k_decay = jnp.exp(log_g[:, None] * (chunk - 1.0 - idx))[:, :, None]
    chunk_decay = jnp.broadcast_to(jnp.exp(log_g * chunk)[:, None, None], (RET_HEADS, 1, RET_DV))
    rows = lambda width: pl.BlockSpec((None, chunk, width), lambda b, c: (b, c, 0))
    state = pl.BlockSpec((None, RET_HEADS, RET_DK, RET_DV), lambda b, c: (b, 0, 0, 0))
    return pl.pallas_call(
        _retention_body,
        out_shape=[jax.ShapeDtypeStruct(v.shape, F32), jax.ShapeDtypeStruct(s0.shape, F32)],
        grid=(bsz, n_chunks),
        in_specs=[rows(RET_QK_WIDTH), rows(RET_QK_WIDTH), rows(RET_V_WIDTH), rows(RET_V_WIDTH), state,
                  _const_spec((RET_HEADS, chunk, chunk)), _const_spec((RET_HEADS, chunk, 1)),
                  _const_spec((RET_HEADS, chunk, 1)), _const_spec((RET_HEADS, 1, RET_DV))],
        out_specs=[rows(RET_V_WIDTH), state],
        scratch_shapes=[pltpu.VMEM((RET_HEADS, RET_DK, RET_DV), F32)],
        compiler_params=_params(2),
        name=f"retention_c{chunk}",
    )(q, k, v, g, s0.astype(F32), decay_mask, q_decay, k_decay, chunk_decay)


def _topk_rank(gate, axis, n):
    idx = lax.broadcasted_iota(jnp.int32, gate.shape, axis)
    rank = jnp.zeros(gate.shape, jnp.int32)
    for other in range(n):
        g_o = gate[:, other:other + 1] if axis == 1 else gate[other:other + 1, :]
        beats = (g_o > gate) | ((g_o == gate) & (other < idx))
        rank = rank + beats.astype(jnp.int32)
    return rank


def _moba_prompt_body(q_ref, k_ref, v_ref, ksum_ref, o_ref, selm_s, *, n_blocks):
    qb = pl.program_id(1)
    own = (qb * MOBA_QBLOCK) // MOBA_BLOCK
    q_off = (qb * MOBA_QBLOCK) % MOBA_BLOCK
    row = lax.broadcasted_iota(jnp.int32, (MOBA_QBLOCK, MOBA_BLOCK), 0)
    col = lax.broadcasted_iota(jnp.int32, (MOBA_QBLOCK, MOBA_BLOCK), 1)
    causal = col <= q_off + row
    blk = lax.broadcasted_iota(jnp.int32, (MOBA_QBLOCK, n_blocks), 1)
    scale = MOBA_HEAD_DIM ** -0.5
    outs = []
    for h in range(MOBA_HEADS):
        hd = slice(h * MOBA_HEAD_DIM, (h + 1) * MOBA_HEAD_DIM)
        qh = q_ref[:, hd]
        kmean = ksum_ref[:, hd] * (1.0 / MOBA_BLOCK)
        gate = lax.dot_general(qh, kmean, NT_DIMS, precision=lax.Precision.HIGHEST,
                               preferred_element_type=F32)
        gate = jnp.where(blk < own, gate, NEG_INF)
        sel = ((_topk_rank(gate, 1, n_blocks) < MOBA_TOPK) & (blk < own)).astype(F32)
        for n in range(n_blocks):
            selm_s[n] = jnp.broadcast_to(sel[:, n:n + 1], (MOBA_QBLOCK, LANES))
        qs = (qh * scale).astype(BF16)

        def block_kv(n):
            rows = pl.ds(pl.multiple_of(n * MOBA_BLOCK, MOBA_BLOCK), MOBA_BLOCK)
            return k_ref[rows, hd], v_ref[rows, hd]

        kn, vn = block_kv(own)
        s = lax.dot_general(qs, kn, NT_DIMS, preferred_element_type=F32)
        s = jnp.where(causal, s, NEG_INF)
        m = jnp.max(s, axis=-1, keepdims=True)
        p = jnp.exp(s - m)
        l = jnp.sum(p, axis=-1, keepdims=True)
        acc = jnp.dot(p.astype(BF16), vn, preferred_element_type=F32)

        def past_block(n, carry):
            m, l, acc = carry
            kn, vn = block_kv(n)
            s = lax.dot_general(qs, kn, NT_DIMS, preferred_element_type=F32)
            chosen = jnp.concatenate([selm_s[n]] * (MOBA_BLOCK // LANES), axis=1) > 0.5
            s = jnp.where(chosen, s, NEG_INF)
            m_new = jnp.maximum(m, jnp.max(s, axis=-1, keepdims=True))
            alpha = jnp.exp(m - m_new)
            p = jnp.exp(s - m_new)
            l = alpha * l + jnp.sum(p, axis=-1, keepdims=True)
            acc = alpha * acc + jnp.dot(p.astype(BF16), vn, preferred_element_type=F32)
            return m_new, l, acc

        m, l, acc = lax.fori_loop(0, own, past_block, (m, l, acc))
        outs.append(acc / l)
    o_ref[...] = jnp.concatenate(outs, axis=1)


def _moba_prompt(q, kb, vb, ksum):
    bsz, seq_len, _ = q.shape
    n_blocks = seq_len // MOBA_BLOCK
    whole = lambda rows: pl.BlockSpec((None, rows, MOBA_WIDTH), lambda b, i: (b, 0, 0))
    qblock = pl.BlockSpec((None, MOBA_QBLOCK, MOBA_WIDTH), lambda b, i: (b, i, 0))
    return pl.pallas_call(
        functools.partial(_moba_prompt_body, n_blocks=n_blocks),
        out_shape=jax.ShapeDtypeStruct(q.shape, F32),
        grid=(bsz, seq_len // MOBA_QBLOCK),
        in_specs=[qblock, whole(seq_len), whole(seq_len), whole(n_blocks)],
        out_specs=qblock,
        scratch_shapes=[pltpu.VMEM((n_blocks, MOBA_QBLOCK, LANES), F32)],
        compiler_params=_params(2),
        name="moba_prompt",
    )(q, kb, vb, ksum)


def _moba_sample_body(pt_ref, q_ref, kn_ref, vn_ref, *rest, n_pages, dec_seq):
    del pt_ref
    pg = SAMPLE_PAGES_PER_STEP
    ck_refs = rest[:pg]
    cv_refs = rest[pg:2 * pg]
    o_ref, s_s, ksum_s, qbd_s, sel_s, m_s, l_s, acc_s = rest[2 * pg:]
    phase = pl.program_id(1)
    step = pl.program_id(2)
    n_steps = n_pages // pg
    n_blocks = n_pages * PAGE_SIZE // MOBA_BLOCK
    pages_per_block = MOBA_BLOCK // PAGE_SIZE
    n_cols = MOBA_HEADS * dec_seq
    scale = MOBA_HEAD_DIM ** -0.5
    r_i = lax.broadcasted_iota(jnp.int32, (n_cols, MOBA_WIDTH), 0)
    c_i = lax.broadcasted_iota(jnp.int32, (n_cols, MOBA_WIDTH), 1)
    head_diag = (r_i // dec_seq) == (c_i // MOBA_HEAD_DIM)
    pad = jnp.zeros((PAGE_SIZE - dec_seq, MOBA_WIDTH), F32)

    @pl.when((phase == 0) & (step == 0))
    def _():
        q_rep = jnp.concatenate([q_ref[...]] * MOBA_HEADS, axis=0)
        qbd_s[...] = jnp.where(head_diag, q_rep, 0.0)
        ksum_s[...] = jnp.zeros_like(ksum_s)

    @pl.when(phase == 0)
    def _():
        qs = (qbd_s[...] * scale).astype(BF16)
        for i in range(pg):
            page = step * pg + i
            kp = ck_refs[i][...]
            b_row = pl.ds(page // pages_per_block, 1)
            ksum_s[b_row, :] = ksum_s[b_row, :] + jnp.sum(kp, axis=0, keepdims=True)
            s_s[page] = lax.dot_general(kp.astype(BF16), qs, NT_DIMS, preferred_element_type=F32)

    @pl.when((phase == 1) & (step == 0))
    def _():
        kmean = ksum_s[...] * (1.0 / MOBA_BLOCK)
        gate = lax.dot_general(kmean, qbd_s[...], NT_DIMS, precision=lax.Precision.HIGHEST,
                               preferred_element_type=F32)
        sel_s[...] = (_topk_rank(gate, 0, n_blocks) < MOBA_TOPK).astype(F32)
        qs = (qbd_s[...] * scale).astype(BF16)
        k_own = jnp.concatenate([kn_ref[...], pad], axis=0).astype(BF16)
        s_own = lax.dot_general(k_own, qs, NT_DIMS, preferred_element_type=F32)
        key_i = lax.broadcasted_iota(jnp.int32, s_own.shape, 0)
        q_i = lax.broadcasted_iota(jnp.int32, s_own.shape, 1) % dec_seq
        s_own = jnp.where(key_i <= q_i, s_own, NEG_INF)

        def page_max(page, m):
            chosen = sel_s[pl.ds(page // pages_per_block, 1), :] > 0.5
            return jnp.maximum(m, jnp.where(chosen, jnp.max(s_s[page], axis=0, keepdims=True), NEG_INF))

        m = lax.fori_loop(0, n_pages, page_max, jnp.max(s_own, axis=0, keepdims=True))
        m_s[...] = m
        p = jnp.exp(s_own - m)
        l_s[...] = jnp.sum(p, axis=0, keepdims=True)
        v_own = jnp.concatenate([vn_ref[...], pad], axis=0).astype(BF16)
        acc_s[...] = lax.dot_general(p.astype(BF16), v_own, TN_DIMS, preferred_element_type=F32)

    @pl.when(phase == 1)
    def _():
        m = m_s[...]
        l = l_s[...]
        acc = acc_s[...]
        for i in range(pg):
            page = step * pg + i
            chosen = sel_s[pl.ds(page // pages_per_block, 1), :] > 0.5
            p = jnp.exp(jnp.where(chosen, s_s[page] - m, NEG_INF))
            l = l + jnp.sum(p, axis=0, keepdims=True)
            acc = acc + lax.dot_general(p.astype(BF16), cv_refs[i][...].astype(BF16), TN_DIMS,
                                        preferred_element_type=F32)
        l_s[...] = l
        acc_s[...] = acc

    @pl.when((phase == 1) & (step == n_steps - 1))
    def _():
        eye = (lax.broadcasted_iota(jnp.int32, (n_cols, n_cols), 0)
               == lax.broadcasted_iota(jnp.int32, (n_cols, n_cols), 1))
        l_col = jnp.sum(jnp.where(eye, jnp.broadcast_to(l_s[...], (n_cols, n_cols)), 0.0),
                        axis=1, keepdims=True)
        acc = jnp.where(head_diag, acc_s[...] / l_col, 0.0)
        out = acc[0:dec_seq]
        for h in range(1, MOBA_HEADS):
            out = out + acc[h * dec_seq:(h + 1) * dec_seq]
        o_ref[...] = out


def _moba_sample(layer, page_table, q, k_new, v_new, cache_k, cache_v):
    bsz, dec_seq, _ = q.shape
    n_pages = page_table.shape[1]
    pg = SAMPLE_PAGES_PER_STEP
    n_steps = n_pages // pg
    n_blocks = n_pages * PAGE_SIZE // MOBA_BLOCK
    n_cols = MOBA_HEADS * dec_seq
    tok = pl.BlockSpec((None, dec_seq, MOBA_WIDTH), lambda b, ph, s, pt: (b, 0, 0))

    def k_spec(i):
        def index(b, ph, s, pt):
            return (layer, pt[b, jnp.where(ph == 0, s, n_steps - 1) * pg + i], 0, 0)
        return pl.BlockSpec((None, None, PAGE_SIZE, MOBA_WIDTH), index)

    def v_spec(i):
        def index(b, ph, s, pt):
            return (layer, pt[b, jnp.where(ph == 0, 0, s) * pg + i], 0, 0)
        return pl.BlockSpec((None, None, PAGE_SIZE, MOBA_WIDTH), index)

    grid_spec = pltpu.PrefetchScalarGridSpec(
        num_scalar_prefetch=1,
        grid=(bsz, 2, n_steps),
        in_specs=[tok, tok, tok] + [k_spec(i) for i in range(pg)] + [v_spec(i) for i in range(pg)],
        out_specs=tok,
        scratch_shapes=[pltpu.VMEM((n_pages, PAGE_SIZE, n_cols), F32),
                        pltpu.VMEM((n_blocks, MOBA_WIDTH), F32),
                        pltpu.VMEM((n_cols, MOBA_WIDTH), F32),
                        pltpu.VMEM((n_blocks, n_cols), F32),
                        pltpu.VMEM((1, n_cols), F32),
                        pltpu.VMEM((1, n_cols), F32),
                        pltpu.VMEM((n_cols, MOBA_WIDTH), F32)],
    )
    return pl.pallas_call(
        functools.partial(_moba_sample_body, n_pages=n_pages, dec_seq=dec_seq),
        out_shape=jax.ShapeDtypeStruct(q.shape, F32),
        grid_spec=grid_spec,
        compiler_params=_params(3),
        name="moba_sample",
    )(page_table, q, k_new, v_new, *([cache_k] * pg), *([cache_v] * pg))


def _merge_body(x_ref, ys_ref, ym_ref, yr_ref, gate_ref, wbs_ref, wbm_ref, wbr_ref, wo_ref, g2_ref,
                wu_ref, wd_ref, gf_ref, o_ref, *, final):
    merged = (gate_ref[:, 0:D_MODEL] * _mm(ys_ref[...], wbs_ref[...])
              + gate_ref[:, D_MODEL:2 * D_MODEL] * _mm(ym_ref[...], wbm_ref[...])
              + gate_ref[:, 2 * D_MODEL:3 * D_MODEL] * _mm(yr_ref[...], wbr_ref[...]))
    x = x_ref[...] + _mm(merged, wo_ref[...])
    up = _mm(_rmsnorm(x, g2_ref[...]), wu_ref[...])
    x = x + _mm(jnp.square(jnp.maximum(up, 0.0)), wd_ref[...])
    if final:
        x = _rmsnorm(x, gf_ref[...])
    o_ref[...] = x


def _merge(x, y_s5, y_moba, y_ret, gates, w, final):
    n = x.shape[0]
    row = lambda width: pl.BlockSpec((ROW_TILE, width), lambda i: (i, 0))
    return pl.pallas_call(
        functools.partial(_merge_body, final=final),
        out_shape=jax.ShapeDtypeStruct(x.shape, F32),
        grid=(n // ROW_TILE,),
        in_specs=[row(D_MODEL), row(S5_WIDTH), row(MOBA_WIDTH), row(RET_V_WIDTH), row(3 * D_MODEL),
                  _const_spec((S5_WIDTH, D_MODEL)), _const_spec((MOBA_WIDTH, D_MODEL)),
                  _const_spec((RET_V_WIDTH, D_MODEL)), _const_spec((D_MODEL, D_MODEL)),
                  _const_spec((1, D_MODEL)), _const_spec((D_MODEL, D_FF)), _const_spec((D_FF, D_MODEL)),
                  _const_spec((1, D_MODEL))],
        out_specs=row(D_MODEL),
        compiler_params=_params(1),
        name="merge_mlp",
    )(x, y_s5, y_moba, y_ret, gates, w["w_br_s5"], w["w_br_moba"], w["w_br_ret"], w["w_out"],
      w["norm2_g"], w["w_mlp_up"], w["w_mlp_down"], w["final_g"])


def _rope_tables(pos):
    half = MOBA_HEAD_DIM // 2
    freqs = ROPE_THETA ** (-jnp.arange(half, dtype=F32) / half)
    ang = pos.astype(F32)[:, None] * freqs[None, :]
    cos = jnp.cos(ang)
    sin = jnp.sin(ang)
    reps = LANES // MOBA_HEAD_DIM
    return (jnp.concatenate([cos, cos] * reps, axis=1), jnp.concatenate([-sin, sin] * reps, axis=1))


def kernel(x_prompt, x_sample, cache_k, cache_v, state_s5_re, state_s5_im, state_ret, page_table,
           norm1_g, w_in, s5_lambda_re, s5_lambda_im, s5_log_dt, s5_b_re, s5_b_im, s5_c_re, s5_c_im,
           s5_d, s5_w_glu, w_br_s5, w_br_moba, w_br_ret, w_out, norm2_g, w_mlp_up, w_mlp_down, final_g):
    bsz_p, seq_p, _ = x_prompt.shape
    bsz_s, seq_s, _ = x_sample.shape
    depth = w_in.shape[0]
    n_pages = page_table.shape[1]
    past_len = n_pages * PAGE_SIZE
    n_p = bsz_p * seq_p
    n_s = bsz_s * seq_s
    prompt_steps = ROW_TILE // SUBLANES
    assert seq_p % ROW_TILE == 0 and n_s % ROW_TILE == 0 and seq_p % RET_CHUNK == 0
    assert past_len % MOBA_BLOCK == 0 and n_pages % SAMPLE_PAGES_PER_STEP == 0
    assert seq_s == SUBLANES and bsz_s % SUBLANES == 0 and seq_s % RET_CHUNK != 0

    cos_p, sin_p = _rope_tables(jnp.tile(jnp.arange(seq_p, dtype=jnp.int32), bsz_p))
    cos_s, sin_s = _rope_tables(jnp.tile(past_len + jnp.arange(seq_s, dtype=jnp.int32), bsz_s))
    cache_k = cache_k.reshape(cache_k.shape[0], cache_k.shape[1], PAGE_SIZE, MOBA_WIDTH)
    cache_v = cache_v.reshape(cache_v.shape[0], cache_v.shape[1], PAGE_SIZE, MOBA_WIDTH)
    zeros_s5 = jnp.zeros((bsz_p, 1, S5_FLAT), F32)
    zeros_ret = jnp.zeros((bsz_p, RET_HEADS, RET_DK, RET_DV), F32)

    hp = x_prompt.reshape(n_p, D_MODEL)
    hs = x_sample.reshape(n_s, D_MODEL)
    outs = {name: [] for name in ("kp", "vp", "ks", "vs", "s5rp", "s5ip", "s5rs", "s5is", "retp", "rets")}
    for l in range(depth):
        last = l == depth - 1
        w = {"w_br_s5": w_br_s5[l].astype(BF16), "w_br_moba": w_br_moba[l].astype(BF16),
             "w_br_ret": w_br_ret[l].astype(BF16), "w_out": w_out[l].astype(BF16),
             "norm2_g": norm2_g[l].reshape(1, D_MODEL), "w_mlp_up": w_mlp_up[l].astype(BF16),
             "w_mlp_down": w_mlp_down[l].astype(BF16), "final_g": final_g.reshape(1, D_MODEL)}
        g1 = norm1_g[l].reshape(1, D_MODEL)
        w_in_l = w_in[l].astype(BF16)
        s5p = _s5_prepare(s5_lambda_re[l], s5_lambda_im[l], s5_log_dt[l], s5_b_re[l], s5_b_im[l],
                          s5_c_re[l], s5_c_im[l], s5_d[l], s5_w_glu[l], (prompt_steps, seq_s))

        (u, qm, km, vm, kb, vb, qr, kr, vr, gr, gates, ksum) = _inproj(hp, g1, w_in_l, cos_p, sin_p)
        seq = lambda t: t.reshape(bsz_p, seq_p, t.shape[-1])
        y_s5, s5r, s5i = _s5_mixer(seq(u), zeros_s5, zeros_s5, s5p, steps=prompt_steps, chained=True)
        y_moba = _moba_prompt(seq(qm), seq(kb), seq(vb), ksum.reshape(bsz_p, seq_p // MOBA_BLOCK, MOBA_WIDTH))
        y_ret, ret_s = _retention(seq(qr), seq(kr), seq(vr), seq(gr), zeros_ret, RET_CHUNK)
        hp = _merge(hp, y_s5.reshape(n_p, -1), y_moba.reshape(n_p, -1), y_ret.reshape(n_p, -1), gates, w, last)
        outs["kp"].append(km.reshape(bsz_p, seq_p, MOBA_HEADS, MOBA_HEAD_DIM))
        outs["vp"].append(vm.reshape(bsz_p, seq_p, MOBA_HEADS, MOBA_HEAD_DIM))
        outs["s5rp"].append(s5r.reshape(bsz_p, S5_GROUPS, S5_STATE))
        outs["s5ip"].append(s5i.reshape(bsz_p, S5_GROUPS, S5_STATE))
        outs["retp"].append(ret_s)

        (u, qm, km, vm, kb, vb, qr, kr, vr, gr, gates, ksum) = _inproj(hs, g1, w_in_l, cos_s, sin_s)
        grp = lambda t: t.reshape(bsz_s // SUBLANES, SUBLANES * seq_s, t.shape[-1])
        st = lambda t: t.astype(F32).reshape(bsz_s // SUBLANES, SUBLANES, S5_FLAT)
        y_s5, s5r, s5i = _s5_mixer(grp(u), st(state_s5_re[l]), st(state_s5_im[l]), s5p,
                                   steps=seq_s, chained=False)
        seq = lambda t: t.reshape(bsz_s, seq_s, t.shape[-1])
        y_moba = _moba_sample(l, page_table, seq(qm), seq(km), seq(vm), cache_k, cache_v)
        y_ret, ret_s = _retention(seq(qr), seq(kr), seq(vr), seq(gr), state_ret[l], seq_s)
        hs = _merge(hs, y_s5.reshape(n_s, -1), y_moba.reshape(n_s, -1), y_ret.reshape(n_s, -1), gates, w, last)
        outs["ks"].append(km.reshape(bsz_s, seq_s, MOBA_HEADS, MOBA_HEAD_DIM))
        outs["vs"].append(vm.reshape(bsz_s, seq_s, MOBA_HEADS, MOBA_HEAD_DIM))
        outs["s5rs"].append(s5r.reshape(bsz_s, S5_GROUPS, S5_STATE))
        outs["s5is"].append(s5i.reshape(bsz_s, S5_GROUPS, S5_STATE))
        outs["rets"].append(ret_s)

    stack = lambda name: jnp.stack(outs[name])
    return (hp.reshape(x_prompt.shape), hs.reshape(x_sample.shape),
            stack("kp"), stack("vp"), stack("ks"), stack("vs"),
            stack("s5rp"), stack("s5ip"), stack("s5rs"), stack("s5is"), stack("retp"), stack("rets"))
```

```python
import functools
import math

import jax
import jax.numpy as jnp
from jax import lax
from jax.experimental import pallas as pl
from jax.experimental.pallas import tpu as pltpu

F32 = jnp.float32
BF16 = jnp.bfloat16

D_MODEL = 1024
PAGE_SIZE = 128
S5_WIDTH = 512
S5_GROUP = 16
S5_GROUPS = 32
S5_STATE = 64
S5_FLAT = S5_GROUPS * S5_STATE
MOBA_HEADS = 8
MOBA_HEAD_DIM = 64
MOBA_WIDTH = 512
MOBA_BLOCK = 256
MOBA_TOPK = 3
MOBA_QBLOCK = 128
RET_HEADS = 4
RET_DK = 64
RET_DV = 128
RET_QK_WIDTH = 256
RET_V_WIDTH = 512
RET_CHUNK = 128
D_FF = 4096
ROPE_THETA = 10000.0
NORM_EPS = 1e-6
NEG_INF = -1e30
IN_SPLITS = (S5_WIDTH, MOBA_WIDTH, MOBA_WIDTH, MOBA_WIDTH, RET_QK_WIDTH, RET_QK_WIDTH,
             RET_V_WIDTH, RET_V_WIDTH, D_MODEL, D_MODEL, D_MODEL)
IN_COLS = sum(IN_SPLITS)
(OFF_U, OFF_QM, OFF_KM, OFF_VM, OFF_QR, OFF_KR, OFF_VR, OFF_GR, OFF_GATES) = (
    0, 512, 1024, 1536, 2048, 2304, 2560, 3072, 3584)

LANES = 128
SUBLANES = 8
ROW_TILE = 256
S5_BUNDLE = 8
S5_LANE_GROUP = 512
SAMPLE_BLOCKS_PER_STEP = 2
VMEM_LIMIT = 56 * 1024 * 1024

NT_DIMS = (((1,), (1,)), ((), ()))
TN_DIMS = (((0,), (0,)), ((), ()))


def _const_spec(shape):
    nd = len(shape)
    return pl.BlockSpec(shape, lambda *_: (0,) * nd, pipeline_mode=pl.Buffered(1))


def _params(n_axes):
    return pltpu.CompilerParams(dimension_semantics=("arbitrary",) * n_axes,
                                vmem_limit_bytes=VMEM_LIMIT)


def _rmsnorm(x, g):
    return x * lax.rsqrt(jnp.mean(x * x, axis=-1, keepdims=True) + NORM_EPS) * g


def _mm(a, w):
    return jnp.dot(a.astype(BF16), w, preferred_element_type=F32)


def _rope(t, cos, sin_signed):
    n = t.shape[1]
    reps = n // LANES
    c = jnp.concatenate([cos] * reps, axis=1)
    s = jnp.concatenate([sin_signed] * reps, axis=1)
    lane = lax.broadcasted_iota(jnp.int32, t.shape, 1)
    first_half = (lane % MOBA_HEAD_DIM) < (MOBA_HEAD_DIM // 2)
    partner = jnp.where(first_half, pltpu.roll(t, n - MOBA_HEAD_DIM // 2, 1),
                        pltpu.roll(t, MOBA_HEAD_DIM // 2, 1))
    return t * c + partner * s


def _inproj_body(x_ref, g_ref, w_ref, cos_ref, sin_ref,
                 u_ref, qm_ref, km_ref, vm_ref, kb_ref, vb_ref, qr_ref, kr_ref, vr_ref, gr_ref,
                 gate_ref, ksum_ref):
    hb = _rmsnorm(x_ref[...], g_ref[...]).astype(BF16)
    cos = cos_ref[...]
    sin = sin_ref[...]

    def proj(lo, n):
        return jnp.dot(hb, w_ref[:, lo:lo + n], preferred_element_type=F32)

    u_ref[...] = proj(OFF_U, S5_WIDTH)
    qm_ref[...] = _rope(proj(OFF_QM, MOBA_WIDTH), cos, sin)
    km = _rope(proj(OFF_KM, MOBA_WIDTH), cos, sin)
    km_ref[...] = km
    kb_ref[...] = km.astype(BF16)
    ksum_ref[0] = jnp.sum(km, axis=0, keepdims=True)
    vm = proj(OFF_VM, MOBA_WIDTH)
    vm_ref[...] = vm
    vb_ref[...] = vm.astype(BF16)
    qr_ref[...] = _rope(proj(OFF_QR, RET_QK_WIDTH), cos, sin)
    kr_ref[...] = _rope(proj(OFF_KR, RET_QK_WIDTH), cos, sin) * (RET_DK ** -0.5)
    vr_ref[...] = proj(OFF_VR, RET_V_WIDTH)
    gr = proj(OFF_GR, RET_V_WIDTH)
    gr_ref[...] = gr * jax.nn.sigmoid(gr)
    for i in range(3):
        gate_ref[:, i * D_MODEL:(i + 1) * D_MODEL] = jax.nn.sigmoid(proj(OFF_GATES + i * D_MODEL, D_MODEL))


def _inproj(x, g, w, cos, sin):
    n = x.shape[0]
    nt = n // ROW_TILE
    row = lambda width: pl.BlockSpec((ROW_TILE, width), lambda i: (i, 0))
    widths = (S5_WIDTH, MOBA_WIDTH, MOBA_WIDTH, MOBA_WIDTH, MOBA_WIDTH, MOBA_WIDTH,
              RET_QK_WIDTH, RET_QK_WIDTH, RET_V_WIDTH, RET_V_WIDTH, 3 * D_MODEL)
    dtypes = (F32, F32, F32, F32, BF16, BF16, F32, F32, F32, F32, F32)
    out_shape = [jax.ShapeDtypeStruct((n, wd), dt) for wd, dt in zip(widths, dtypes)]
    out_shape.append(jax.ShapeDtypeStruct((nt, 1, MOBA_WIDTH), F32))
    out_specs = [row(wd) for wd in widths]
    out_specs.append(pl.BlockSpec((1, 1, MOBA_WIDTH), lambda i: (i, 0, 0)))
    return pl.pallas_call(
        _inproj_body,
        out_shape=out_shape,
        grid=(nt,),
        in_specs=[row(D_MODEL), _const_spec((1, D_MODEL)), _const_spec((D_MODEL, IN_COLS)),
                  row(LANES), row(LANES)],
        out_specs=out_specs,
        compiler_params=_params(1),
        name="inproj",
    )(x, g, w, cos, sin)


def _gelu_tanh(x):
    cdf = 0.5 * (1.0 + jnp.tanh(math.sqrt(2.0 / math.pi) * (x + 0.044715 * (x * x * x))))
    return x * cdf


def _s5_body(u_ref, x0r_ref, x0i_ref, lamr_ref, lami_ref, ljr_ref, lji_ref, wbr_ref, wbi_ref,
             wcr_ref, wci_ref, d_ref, wg_ref,
             y_ref, xlr_ref, xli_ref,
             xr_s, xi_s, cr_s, ci_s, carr_s, cari_s, *, steps, chained):
    n_bundles = S5_GROUPS // S5_BUNDLE
    in_w = S5_BUNDLE * S5_GROUP
    st_w = S5_BUNDLE * S5_STATE
    u = u_ref[...]
    ub = u.astype(BF16)
    for j in range(n_bundles):
        uj = ub[:, j * in_w:(j + 1) * in_w]
        xr_s[:, j * st_w:(j + 1) * st_w] = jnp.dot(uj, wbr_ref[j], preferred_element_type=F32)
        xi_s[:, j * st_w:(j + 1) * st_w] = jnp.dot(uj, wbi_ref[j], preferred_element_type=F32)

    def scan(init_r, init_i, lo, store):
        lanes = slice(lo, lo + S5_LANE_GROUP)
        lr = lamr_ref[:, lanes]
        li = lami_ref[:, lanes]

        def step(t, carry):
            xr, xi = carry
            rows = pl.ds(pl.multiple_of(t * SUBLANES, SUBLANES), SUBLANES)
            nr = lr * xr - li * xi + xr_s[rows, lanes]
            ni = lr * xi + li * xr + xi_s[rows, lanes]
            if store:
                xr_s[rows, lanes] = nr
                xi_s[rows, lanes] = ni
            return nr, ni

        return lax.fori_loop(0, steps, step, (init_r, init_i), unroll=4)

    if chained:
        @pl.when(pl.program_id(1) == 0)
        def _():
            carr_s[...] = x0r_ref[...]
            cari_s[...] = x0i_ref[...]

    zeros = jnp.zeros((SUBLANES, S5_LANE_GROUP), F32)
    for lo in range(0, S5_FLAT, S5_LANE_GROUP):
        lanes = slice(lo, lo + S5_LANE_GROUP)
        if chained:
            end_r, end_i = scan(zeros, zeros, lo, store=False)
            c_r = carr_s[:, lanes]
            c_i = cari_s[:, lanes]
            ljr = ljr_ref[:, lanes]
            lji = lji_ref[:, lanes]
            for s in range(SUBLANES):
                cr_s[s:s + 1, lanes] = c_r
                ci_s[s:s + 1, lanes] = c_i
                n_r = ljr * c_r - lji * c_i + end_r[s:s + 1, :]
                n_i = ljr * c_i + lji * c_r + end_i[s:s + 1, :]
                c_r, c_i = n_r, n_i
            carr_s[:, lanes] = c_r
            cari_s[:, lanes] = c_i
            scan(cr_s[:, lanes], ci_s[:, lanes], lo, store=True)
        else:
            f_r, f_i = scan(x0r_ref[:, lanes], x0i_ref[:, lanes], lo, store=True)
            xlr_ref[:, lanes] = f_r
            xli_ref[:, lanes] = f_i
    if chained:
        xlr_ref[...] = carr_s[...]
        xli_ref[...] = cari_s[...]

    ys = []
    for j in range(n_bundles):
        sl = slice(j * st_w, (j + 1) * st_w)
        ys.append(_mm(xr_s[:, sl], wcr_ref[j]) + _mm(xi_s[:, sl], wci_ref[j]))
    y = jnp.concatenate(ys, axis=1) + d_ref[...] * u
    y = _gelu_tanh(y)
    y_ref[...] = y * jax.nn.sigmoid(_mm(y, wg_ref[...]))


def _s5_mixer(u, x0r, x0i, prm, *, steps, chained):
    n_groups, rows, _ = u.shape
    t = SUBLANES * steps
    n_chunks = rows // t

    def swap_row_order(a, inner, outer):
        return a.reshape(n_groups, n_chunks, inner, outer, -1).transpose(0, 1, 3, 2, 4).reshape(a.shape)

    u = swap_row_order(u, SUBLANES, steps)
    r0 = x0r.shape[1]
    chunk = pl.BlockSpec((None, t, S5_WIDTH), lambda b, c: (b, c, 0))
    state = pl.BlockSpec((None, r0, S5_FLAT), lambda b, c: (b, 0, 0))
    n_bundles = S5_GROUPS // S5_BUNDLE
    in_w = S5_BUNDLE * S5_GROUP
    st_w = S5_BUNDLE * S5_STATE
    body = functools.partial(_s5_body, steps=steps, chained=chained)
    y, xl_r, xl_i = pl.pallas_call(
        body,
        out_shape=[jax.ShapeDtypeStruct(u.shape, F32),
                   jax.ShapeDtypeStruct(x0r.shape, F32), jax.ShapeDtypeStruct(x0r.shape, F32)],
        grid=(n_groups, n_chunks),
        in_specs=[chunk, state, state,
                  _const_spec((SUBLANES, S5_FLAT)), _const_spec((SUBLANES, S5_FLAT)),
                  _const_spec((1, S5_FLAT)), _const_spec((1, S5_FLAT)),
                  _const_spec((n_bundles, in_w, st_w)), _const_spec((n_bundles, in_w, st_w)),
                  _const_spec((n_bundles, st_w, in_w)), _const_spec((n_bundles, st_w, in_w)),
                  _const_spec((1, S5_WIDTH)), _const_spec((S5_WIDTH, S5_WIDTH))],
        out_specs=[chunk, state, state],
        scratch_shapes=[pltpu.VMEM((t, S5_FLAT), F32), pltpu.VMEM((t, S5_FLAT), F32),
                        pltpu.VMEM((SUBLANES, S5_FLAT), F32), pltpu.VMEM((SUBLANES, S5_FLAT), F32),
                        pltpu.VMEM((1, S5_FLAT), F32), pltpu.VMEM((1, S5_FLAT), F32)],
        compiler_params=_params(2),
        name="s5_chained" if chained else "s5_independent",
    )(u, x0r, x0i, prm["lam_r8"], prm["lam_i8"], prm["lj_r"][steps], prm["lj_i"][steps],
      prm["wb_r"], prm["wb_i"], prm["wc_r"], prm["wc_i"], prm["d"], prm["w_glu"])
    return swap_row_order(y, steps, SUBLANES), xl_r, xl_i


def _s5_prepare(lam_re, lam_im, log_dt, b_re, b_im, c_re, c_im, d_skip, w_glu, step_counts):
    a = jnp.minimum(lam_re.astype(F32), -1e-4)
    b = lam_im.astype(F32)
    dt = jnp.exp(log_dt.astype(F32))[:, None]

    def lam_bar_power(n):
        mag = jnp.exp(a * dt * n)
        return mag * jnp.cos(b * dt * n), mag * jnp.sin(b * dt * n)

    e_r, e_i = lam_bar_power(1.0)
    den = a * a + b * b
    cf_r = ((e_r - 1.0) * a + e_i * b) / den
    cf_i = (e_i * a - (e_r - 1.0) * b) / den
    bb_r = cf_r[..., None] * b_re.astype(F32) - cf_i[..., None] * b_im.astype(F32)
    bb_i = cf_r[..., None] * b_im.astype(F32) + cf_i[..., None] * b_re.astype(F32)
    n_bundles = S5_GROUPS // S5_BUNDLE
    eye = jnp.eye(S5_BUNDLE, dtype=F32)

    def expand_b(t):
        t = t.reshape(n_bundles, S5_BUNDLE, S5_STATE, S5_GROUP)
        w = jnp.einsum("jgpc,gh->jgchp", t, eye)
        return w.reshape(n_bundles, S5_BUNDLE * S5_GROUP, S5_BUNDLE * S5_STATE).astype(BF16)

    def expand_c(t):
        t = t.reshape(n_bundles, S5_BUNDLE, S5_GROUP, S5_STATE)
        w = jnp.einsum("jgcp,gh->jgphc", t, eye)
        return w.reshape(n_bundles, S5_BUNDLE * S5_STATE, S5_BUNDLE * S5_GROUP).astype(BF16)

    flat = lambda t: t.reshape(1, S5_FLAT)
    lj_r, lj_i = {}, {}
    for steps in step_counts:
        p_r, p_i = lam_bar_power(float(steps))
        lj_r[steps] = flat(p_r)
        lj_i[steps] = flat(p_i)
    return {
        "lam_r8": jnp.broadcast_to(flat(e_r), (SUBLANES, S5_FLAT)),
        "lam_i8": jnp.broadcast_to(flat(e_i), (SUBLANES, S5_FLAT)),
        "lj_r": lj_r, "lj_i": lj_i,
        "wb_r": expand_b(bb_r), "wb_i": expand_b(bb_i),
        "wc_r": expand_c(c_re.astype(F32)), "wc_i": expand_c(-c_im.astype(F32)),
        "d": d_skip.astype(F32).reshape(1, S5_WIDTH),
        "w_glu": w_glu.astype(BF16),
    }


def _retention_body(q_ref, k_ref, v_ref, g_ref, s0_ref, dm_ref, qd_ref, kd_ref, cd_ref,
                    y_ref, sout_ref, s_s):
    @pl.when(pl.program_id(1) == 0)
    def _():
        s_s[...] = s0_ref[...]

    for h in range(RET_HEADS):
        qk = slice(h * RET_DK, (h + 1) * RET_DK)
        vv = slice(h * RET_DV, (h + 1) * RET_DV)
        q = q_ref[:, qk].astype(BF16)
        k = k_ref[:, qk]
        v = v_ref[:, vv].astype(BF16)
        s = s_s[h]
        inner = lax.dot_general(q, k.astype(BF16), NT_DIMS, preferred_element_type=F32) * dm_ref[h]
        o = (jnp.dot(inner.astype(BF16), v, preferred_element_type=F32)
             + jnp.dot(q, s.astype(BF16), preferred_element_type=F32) * qd_ref[h])
        k_dec = (k * kd_ref[h]).astype(BF16)
        s_s[h] = s * cd_ref[h] + lax.dot_general(k_dec, v, TN_DIMS, preferred_element_type=F32)
        mu = jnp.mean(o, axis=-1, keepdims=True)
        var = jnp.mean(jnp.square(o - mu), axis=-1, keepdims=True)
        y_ref[:, vv] = g_ref[:, vv] * ((o - mu) * lax.rsqrt(var + NORM_EPS))
    sout_ref[...] = s_s[...]


def _retention(q, k, v, g, s0, chunk):
    bsz, seq_len, _ = q.shape
    n_chunks = seq_len // chunk
    log_g = jnp.log(1.0 - 2.0 ** (-5.0 - jnp.arange(RET_HEADS, dtype=F32)))
    idx = jnp.arange(chunk, dtype=F32)
    diff = idx[:, None] - idx[None, :]
    decay_mask = jnp.where(diff >= 0, jnp.exp(log_g[:, None, None] * jnp.maximum(diff, 0.0)), 0.0)
    q_decay = jnp.exp(log_g[:, None] * (idx + 1.0))[:, :, None]
    k_decay = jnp.exp(log_g[:, None] * (chunk - 1.0 - idx))[:, :, None]
    chunk_decay = jnp.broadcast_to(jnp.exp(log_g * chunk)[:, None, None], (RET_HEADS, 1, RET_DV))
    rows = lambda width: pl.BlockSpec((None, chunk, width), lambda b, c: (b, c, 0))
    state = pl.BlockSpec((None, RET_HEADS, RET_DK, RET_DV), lambda b, c: (b, 0, 0, 0))
    return pl.pallas_call(
        _retention_body,
        out_shape=[jax.ShapeDtypeStruct(v.shape, F32), jax.ShapeDtypeStruct(s0.shape, F32)],
        grid=(bsz, n_chunks),
        in_specs=[rows(RET_QK_WIDTH), rows(RET_QK_WIDTH), rows(RET_V_WIDTH), rows(RET_V_WIDTH), state,
                  _const_spec((RET_HEADS, chunk, chunk)), _const_spec((RET_HEADS, chunk, 1)),
                  _const_spec((RET_HEADS, chunk, 1)), _const_spec((RET_HEADS, 1, RET_DV))],
        out_specs=[rows(RET_V_WIDTH), state],
        scratch_shapes=[pltpu.VMEM((RET_HEADS, RET_DK, RET_DV), F32)],
        compiler_params=_params(2),
        name=f"retention_c{chunk}",
    )(q, k, v, g, s0.astype(F32), decay_mask, q_decay, k_decay, chunk_decay)


def _topk_rank(gate, axis, n):
    idx = lax.broadcasted_iota(jnp.int32, gate.shape, axis)
    rank = jnp.zeros(gate.shape, jnp.int32)
    for other in range(n):
        g_o = gate[:, other:other + 1] if axis == 1 else gate[other:other + 1, :]
        beats = (g_o > gate) | ((g_o == gate) & (other < idx))
        rank = rank + beats.astype(jnp.int32)
    return rank


def _moba_prompt_body(q_ref, k_ref, vt_ref, ksum_ref, o_ref, w_s, sel_s, s_s, p_s, acc_s, *, n_blocks):
    qb = pl.program_id(1)
    own = (qb * MOBA_QBLOCK) // MOBA_BLOCK
    q_off = (qb * MOBA_QBLOCK) % MOBA_BLOCK
    key_i = lax.broadcasted_iota(jnp.int32, (MOBA_BLOCK, MOBA_QBLOCK), 0)
    q_i = lax.broadcasted_iota(jnp.int32, (MOBA_BLOCK, MOBA_QBLOCK), 1)
    causal = key_i <= q_off + q_i
    blk = lax.broadcasted_iota(jnp.int32, (n_blocks, MOBA_QBLOCK), 0)
    scale = MOBA_HEAD_DIM ** -0.5
    q_t = q_ref[...].T
    zero_half = jnp.zeros((MOBA_HEAD_DIM, MOBA_QBLOCK), F32)
    for h in range(MOBA_HEADS):
        hd = slice(h * MOBA_HEAD_DIM, (h + 1) * MOBA_HEAD_DIM)
        q_h = q_t[hd, :]
        kmean = ksum_ref[:, hd] * (1.0 / MOBA_BLOCK)
        gate = jnp.dot(kmean, q_h, precision=lax.Precision.HIGHEST, preferred_element_type=F32)
        gate = jnp.where(blk < own, gate, NEG_INF)
        sel_s[h] = ((_topk_rank(gate, 0, n_blocks) < MOBA_TOPK) & (blk < own)).astype(F32)
        halves = [q_h * scale, zero_half] if h % 2 == 0 else [zero_half, q_h * scale]
        w_s[h] = jnp.concatenate(halves, axis=0).astype(BF16)
    acc_s[...] = jnp.zeros(acc_s.shape, F32)

    def block_update(n, allowed_of, m, l):
        rows = pl.ds(pl.multiple_of(n * MOBA_BLOCK, MOBA_BLOCK), MOBA_BLOCK)
        for h in range(MOBA_HEADS):
            pair = slice((h // 2) * LANES, (h // 2 + 1) * LANES)
            s_s[h] = jnp.dot(k_ref[rows, pair], w_s[h], preferred_element_type=F32)
        m_out, l_out, alphas = [], [], []
        for h in range(MOBA_HEADS):
            s = jnp.where(allowed_of(h), s_s[h], NEG_INF)
            m_new = jnp.maximum(m[h], jnp.max(s, axis=0, keepdims=True))
            alpha = jnp.exp(m[h] - m_new)
            p = jnp.exp(s - m_new)
            p_s[h] = p.astype(BF16)
            l_out.append(alpha * l[h] + jnp.sum(p, axis=0, keepdims=True))
            m_out.append(m_new)
            alphas.append(alpha)
        for h in range(MOBA_HEADS):
            hd = slice(h * MOBA_HEAD_DIM, (h + 1) * MOBA_HEAD_DIM)
            acc_s[hd, :] = alphas[h] * acc_s[hd, :] + jnp.dot(vt_ref[n, hd, :], p_s[h],
                                                              preferred_element_type=F32)
        return tuple(m_out), tuple(l_out)

    m0 = tuple(jnp.full((1, MOBA_QBLOCK), NEG_INF, F32) for _ in range(MOBA_HEADS))
    l0 = tuple(jnp.zeros((1, MOBA_QBLOCK), F32) for _ in range(MOBA_HEADS))
    m, l = block_update(own, lambda h: causal, m0, l0)

    def past_block(n, carry):
        def allowed_of(h):
            return jnp.broadcast_to(sel_s[h, pl.ds(n, 1), :] > 0.5, (MOBA_BLOCK, MOBA_QBLOCK))
        return block_update(n, allowed_of, *carry)

    m, l = lax.fori_loop(0, own, past_block, (m, l))
    outs = []
    for h in range(MOBA_HEADS):
        hd = slice(h * MOBA_HEAD_DIM, (h + 1) * MOBA_HEAD_DIM)
        outs.append(acc_s[hd, :] / l[h])
    o_ref[...] = jnp.concatenate(outs, axis=0).T


def _moba_prompt(q, kb, vt, ksum):
    bsz, seq_len, _ = q.shape
    n_blocks = seq_len // MOBA_BLOCK
    qblock = pl.BlockSpec((None, MOBA_QBLOCK, MOBA_WIDTH), lambda b, i: (b, i, 0))
    return pl.pallas_call(
        functools.partial(_moba_prompt_body, n_blocks=n_blocks),
        out_shape=jax.ShapeDtypeStruct(q.shape, F32),
        grid=(bsz, seq_len // MOBA_QBLOCK),
        in_specs=[qblock,
                  pl.BlockSpec((None, seq_len, MOBA_WIDTH), lambda b, i: (b, 0, 0)),
                  pl.BlockSpec((None, n_blocks, MOBA_WIDTH, MOBA_BLOCK), lambda b, i: (b, 0, 0, 0)),
                  pl.BlockSpec((None, n_blocks, MOBA_WIDTH), lambda b, i: (b, 0, 0))],
        out_specs=qblock,
        scratch_shapes=[pltpu.VMEM((MOBA_HEADS, LANES, MOBA_QBLOCK), BF16),
                        pltpu.VMEM((MOBA_HEADS, n_blocks, MOBA_QBLOCK), F32),
                        pltpu.VMEM((MOBA_HEADS, MOBA_BLOCK, MOBA_QBLOCK), F32),
                        pltpu.VMEM((MOBA_HEADS, MOBA_BLOCK, MOBA_QBLOCK), BF16),
                        pltpu.VMEM((MOBA_WIDTH, MOBA_QBLOCK), F32)],
        compiler_params=_params(2),
        name="moba_prompt",
    )(q, kb, vt, ksum)


def _split_heads(t):
    return jnp.concatenate([t[:, h * MOBA_HEAD_DIM:(h + 1) * MOBA_HEAD_DIM] for h in range(MOBA_HEADS)], axis=0)


def _moba_sample_body(pt_ref, q_ref, kn_ref, vn_ref, *rest, n_pages, dec_seq):
    del pt_ref
    pages_per_block = MOBA_BLOCK // PAGE_SIZE
    pps = SAMPLE_BLOCKS_PER_STEP * pages_per_block
    ck_refs = rest[:pps]
    cv_refs = rest[pps:2 * pps]
    o_ref, qc_s, ksum_s, m_s, l_s, acc_s = rest[2 * pps:]
    step = pl.program_id(1)
    n_steps = n_pages // pps
    n_blocks = n_pages // pages_per_block
    n_cols = MOBA_HEADS * dec_seq
    block_rows = MOBA_BLOCK * MOBA_HEADS
    scale = MOBA_HEAD_DIM ** -0.5

    @pl.when(step == 0)
    def _():
        qc_s[...] = _split_heads(q_ref[...])
        m_s[...] = jnp.full(m_s.shape, NEG_INF, F32)
        l_s[...] = jnp.zeros(l_s.shape, F32)

    qs = (qc_s[...] * scale).astype(BF16)
    row = lax.broadcasted_iota(jnp.int32, (n_cols, block_rows), 0)
    lane = lax.broadcasted_iota(jnp.int32, (n_cols, block_rows), 1)
    same_head = (lane % MOBA_HEADS) == (row // dec_seq)
    blk_lane = lax.broadcasted_iota(jnp.int32, (n_cols, n_blocks), 1)
    for j in range(SAMPLE_BLOCKS_PER_STEP):
        block = step * SAMPLE_BLOCKS_PER_STEP + j
        k_pages = [ck_refs[j * pages_per_block + i][...] for i in range(pages_per_block)]
        v_pages = [cv_refs[j * pages_per_block + i][...] for i in range(pages_per_block)]
        ksum = k_pages[0].sum(axis=0)
        for kp in k_pages[1:]:
            ksum = ksum + kp.sum(axis=0)
        ksum_s[block] = ksum
        kf = jnp.concatenate([kp.reshape(PAGE_SIZE * MOBA_HEADS, MOBA_HEAD_DIM) for kp in k_pages], axis=0)
        vf = jnp.concatenate([vp.reshape(PAGE_SIZE * MOBA_HEADS, MOBA_HEAD_DIM) for vp in v_pages], axis=0)
        s = lax.dot_general(qs, kf.astype(BF16), NT_DIMS, preferred_element_type=F32)
        s = jnp.where(same_head, s, NEG_INF)
        m_n = jnp.max(s, axis=1, keepdims=True)
        p = jnp.exp(s - m_n)
        l_n = jnp.sum(p, axis=1, keepdims=True)
        acc_s[block] = jnp.dot(p.astype(BF16), vf.astype(BF16), preferred_element_type=F32)
        m_s[...] = jnp.where(blk_lane == block, m_n, m_s[...])
        l_s[...] = jnp.where(blk_lane == block, l_n, l_s[...])

    @pl.when(step == n_steps - 1)
    def _():
        qc = qc_s[...]
        kmean = (ksum_s[...] * (1.0 / MOBA_BLOCK)).reshape(n_blocks * MOBA_HEADS, MOBA_HEAD_DIM)
        g_all = lax.dot_general(qc, kmean, NT_DIMS, precision=lax.Precision.HIGHEST,
                                preferred_element_type=F32)
        g_row = lax.broadcasted_iota(jnp.int32, g_all.shape, 0)
        g_lane = lax.broadcasted_iota(jnp.int32, g_all.shape, 1)
        g_all = jnp.where((g_lane % MOBA_HEADS) == (g_row // dec_seq), g_all, 0.0)
        e_row = lax.broadcasted_iota(jnp.int32, (n_blocks * MOBA_HEADS, n_blocks), 0)
        e_col = lax.broadcasted_iota(jnp.int32, (n_blocks * MOBA_HEADS, n_blocks), 1)
        gate = jnp.dot(g_all, ((e_row // MOBA_HEADS) == e_col).astype(F32),
                       precision=lax.Precision.HIGHEST, preferred_element_type=F32)
        sel = _topk_rank(gate, 1, n_blocks) < MOBA_TOPK
        s_own = lax.dot_general(qs, _split_heads(kn_ref[...]).astype(BF16), NT_DIMS,
                                preferred_element_type=F32)
        o_row = lax.broadcasted_iota(jnp.int32, s_own.shape, 0)
        o_lane = lax.broadcasted_iota(jnp.int32, s_own.shape, 1)
        allowed = ((o_lane // dec_seq) == (o_row // dec_seq)) & ((o_lane % dec_seq) <= (o_row % dec_seq))
        s_own = jnp.where(allowed, s_own, NEG_INF)
        m_own = jnp.max(s_own, axis=1, keepdims=True)
        p_own = jnp.exp(s_own - m_own)
        l_own = jnp.sum(p_own, axis=1, keepdims=True)
        acc_own = jnp.dot(p_own.astype(BF16), _split_heads(vn_ref[...]).astype(BF16),
                          preferred_element_type=F32)
        m_all = m_s[...]
        top = jnp.maximum(m_own, jnp.max(jnp.where(sel, m_all, NEG_INF), axis=1, keepdims=True))
        w = jnp.exp(jnp.where(sel, m_all - top, NEG_INF))
        w_own = jnp.exp(m_own - top)
        denom = w_own * l_own + jnp.sum(w * l_s[...], axis=1, keepdims=True)
        out = w_own * acc_own
        for n in range(n_blocks):
            out = out + w[:, n:n + 1] * acc_s[n]
        out = out / denom
        o_ref[...] = jnp.concatenate([out[h * dec_seq:(h + 1) * dec_seq, :] for h in range(MOBA_HEADS)],
                                     axis=1)


def _moba_sample(layer, page_table, q, k_new, v_new, cache_k, cache_v):
    bsz, dec_seq, _ = q.shape
    n_pages = page_table.shape[1]
    pages_per_block = MOBA_BLOCK // PAGE_SIZE
    pps = SAMPLE_BLOCKS_PER_STEP * pages_per_block
    n_steps = n_pages // pps
    n_blocks = n_pages // pages_per_block
    n_cols = MOBA_HEADS * dec_seq
    tok = pl.BlockSpec((None, dec_seq, MOBA_WIDTH), lambda b, s, pt: (b, 0, 0))

    def page_spec(i):
        return pl.BlockSpec((None, None, PAGE_SIZE, MOBA_HEADS, MOBA_HEAD_DIM),
                            lambda b, s, pt: (layer, pt[b, s * pps + i], 0, 0, 0))

    grid_spec = pltpu.PrefetchScalarGridSpec(
        num_scalar_prefetch=1,
        grid=(bsz, n_steps),
        in_specs=[tok, tok, tok] + [page_spec(i) for i in range(pps)] * 2,
        out_specs=tok,
        scratch_shapes=[pltpu.VMEM((n_cols, MOBA_HEAD_DIM), F32),
                        pltpu.VMEM((n_blocks, MOBA_HEADS, MOBA_HEAD_DIM), F32),
                        pltpu.VMEM((n_cols, n_blocks), F32),
                        pltpu.VMEM((n_cols, n_blocks), F32),
                        pltpu.VMEM((n_blocks, n_cols, MOBA_HEAD_DIM), F32)],
    )
    return pl.pallas_call(
        functools.partial(_moba_sample_body, n_pages=n_pages, dec_seq=dec_seq),
        out_shape=jax.ShapeDtypeStruct(q.shape, F32),
        grid_spec=grid_spec,
        compiler_params=_params(2),
        name="moba_sample",
    )(page_table, q, k_new, v_new, *([cache_k] * pps), *([cache_v] * pps))


def _merge_body(x_ref, ys_ref, ym_ref, yr_ref, gate_ref, wbs_ref, wbm_ref, wbr_ref, wo_ref, g2_ref,
                wu_ref, wd_ref, gf_ref, o_ref, *, final):
    merged = (gate_ref[:, 0:D_MODEL] * _mm(ys_ref[...], wbs_ref[...])
              + gate_ref[:, D_MODEL:2 * D_MODEL] * _mm(ym_ref[...], wbm_ref[...])
              + gate_ref[:, 2 * D_MODEL:3 * D_MODEL] * _mm(yr_ref[...], wbr_ref[...]))
    x = x_ref[...] + _mm(merged, wo_ref[...])
    up = _mm(_rmsnorm(x, g2_ref[...]), wu_ref[...])
    x = x + _mm(jnp.square(jnp.maximum(up, 0.0)), wd_ref[...])
    if final:
        x = _rmsnorm(x, gf_ref[...])
    o_ref[...] = x


def _merge(x, y_s5, y_moba, y_ret, gates, w, final):
    n = x.shape[0]
    row = lambda width: pl.BlockSpec((ROW_TILE, width), lambda i: (i, 0))
    return pl.pallas_call(
        functools.partial(_merge_body, final=final),
        out_shape=jax.ShapeDtypeStruct(x.shape, F32),
        grid=(n // ROW_TILE,),
        in_specs=[row(D_MODEL), row(S5_WIDTH), row(MOBA_WIDTH), row(RET_V_WIDTH), row(3 * D_MODEL),
                  _const_spec((S5_WIDTH, D_MODEL)), _const_spec((MOBA_WIDTH, D_MODEL)),
                  _const_spec((RET_V_WIDTH, D_MODEL)), _const_spec((D_MODEL, D_MODEL)),
                  _const_spec((1, D_MODEL)), _const_spec((D_MODEL, D_FF)), _const_spec((D_FF, D_MODEL)),
                  _const_spec((1, D_MODEL))],
        out_specs=row(D_MODEL),
        compiler_params=_params(1),
        name="merge_mlp",
    )(x, y_s5, y_moba, y_ret, gates, w["w_br_s5"], w["w_br_moba"], w["w_br_ret"], w["w_out"],
      w["norm2_g"], w["w_mlp_up"], w["w_mlp_down"], w["final_g"])


def _rope_tables(pos):
    half = MOBA_HEAD_DIM // 2
    freqs = ROPE_THETA ** (-jnp.arange(half, dtype=F32) / half)
    ang = pos.astype(F32)[:, None] * freqs[None, :]
    cos = jnp.cos(ang)
    sin = jnp.sin(ang)
    reps = LANES // MOBA_HEAD_DIM
    return (jnp.concatenate([cos, cos] * reps, axis=1), jnp.concatenate([-sin, sin] * reps, axis=1))


def kernel(x_prompt, x_sample, cache_k, cache_v, state_s5_re, state_s5_im, state_ret, page_table,
           norm1_g, w_in, s5_lambda_re, s5_lambda_im, s5_log_dt, s5_b_re, s5_b_im, s5_c_re, s5_c_im,
           s5_d, s5_w_glu, w_br_s5, w_br_moba, w_br_ret, w_out, norm2_g, w_mlp_up, w_mlp_down, final_g):
    bsz_p, seq_p, _ = x_prompt.shape
    bsz_s, seq_s, _ = x_sample.shape
    depth = w_in.shape[0]
    n_pages = page_table.shape[1]
    past_len = n_pages * PAGE_SIZE
    n_p = bsz_p * seq_p
    n_s = bsz_s * seq_s
    prompt_steps = ROW_TILE // SUBLANES
    assert seq_p % ROW_TILE == 0 and n_s % ROW_TILE == 0 and seq_p % RET_CHUNK == 0
    assert n_pages % (SAMPLE_BLOCKS_PER_STEP * MOBA_BLOCK // PAGE_SIZE) == 0
    assert seq_s == SUBLANES and bsz_s % SUBLANES == 0 and seq_s % RET_CHUNK != 0

    cos_p, sin_p = _rope_tables(jnp.tile(jnp.arange(seq_p, dtype=jnp.int32), bsz_p))
    cos_s, sin_s = _rope_tables(jnp.tile(past_len + jnp.arange(seq_s, dtype=jnp.int32), bsz_s))
    zeros_s5 = jnp.zeros((bsz_p, 1, S5_FLAT), F32)
    zeros_ret = jnp.zeros((bsz_p, RET_HEADS, RET_DK, RET_DV), F32)

    hp = x_prompt.reshape(n_p, D_MODEL)
    hs = x_sample.reshape(n_s, D_MODEL)
    outs = {name: [] for name in ("kp", "vp", "ks", "vs", "s5rp", "s5ip", "s5rs", "s5is", "retp", "rets")}
    for l in range(depth):
        last = l == depth - 1
        w = {"w_br_s5": w_br_s5[l].astype(BF16), "w_br_moba": w_br_moba[l].astype(BF16),
             "w_br_ret": w_br_ret[l].astype(BF16), "w_out": w_out[l].astype(BF16),
             "norm2_g": norm2_g[l].reshape(1, D_MODEL), "w_mlp_up": w_mlp_up[l].astype(BF16),
             "w_mlp_down": w_mlp_down[l].astype(BF16), "final_g": final_g.reshape(1, D_MODEL)}
        g1 = norm1_g[l].reshape(1, D_MODEL)
        w_in_l = w_in[l].astype(BF16)
        s5p = _s5_prepare(s5_lambda_re[l], s5_lambda_im[l], s5_log_dt[l], s5_b_re[l], s5_b_im[l],
                          s5_c_re[l], s5_c_im[l], s5_d[l], s5_w_glu[l], (prompt_steps, seq_s))

        (u, qm, km, vm, kb, vb, qr, kr, vr, gr, gates, ksum) = _inproj(hp, g1, w_in_l, cos_p, sin_p)
        seq = lambda t: t.reshape(bsz_p, seq_p, t.shape[-1])
        y_s5, s5r, s5i = _s5_mixer(seq(u), zeros_s5, zeros_s5, s5p, steps=prompt_steps, chained=True)
        n_blk = seq_p // MOBA_BLOCK
        vt = vb.reshape(bsz_p, n_blk, MOBA_BLOCK, MOBA_WIDTH).transpose(0, 1, 3, 2)
        y_moba = _moba_prompt(seq(qm), seq(kb), vt, ksum.reshape(bsz_p, n_blk, MOBA_WIDTH))
        y_ret, ret_s = _retention(seq(qr), seq(kr), seq(vr), seq(gr), zeros_ret, RET_CHUNK)
        hp = _merge(hp, y_s5.reshape(n_p, -1), y_moba.reshape(n_p, -1), y_ret.reshape(n_p, -1), gates, w, last)
        outs["kp"].append(km.reshape(bsz_p, seq_p, MOBA_HEADS, MOBA_HEAD_DIM))
        outs["vp"].append(vm.reshape(bsz_p, seq_p, MOBA_HEADS, MOBA_HEAD_DIM))
        outs["s5rp"].append(s5r.reshape(bsz_p, S5_GROUPS, S5_STATE))
        outs["s5ip"].append(s5i.reshape(bsz_p, S5_GROUPS, S5_STATE))
        outs["retp"].append(ret_s)

        (u, qm, km, vm, kb, vb, qr, kr, vr, gr, gates, ksum) = _inproj(hs, g1, w_in_l, cos_s, sin_s)
        grp = lambda t: t.reshape(bsz_s // SUBLANES, SUBLANES * seq_s, t.shape[-1])
        st = lambda t: t.astype(F32).reshape(bsz_s // SUBLANES, SUBLANES, S5_FLAT)
        y_s5, s5r, s5i = _s5_mixer(grp(u), st(state_s5_re[l]), st(state_s5_im[l]), s5p,
                                   steps=seq_s, chained=False)
        seq = lambda t: t.reshape(bsz_s, seq_s, t.shape[-1])
        y_moba = _moba_sample(l, page_table, seq(qm), seq(km), seq(vm), cache_k, cache_v)
        y_ret, ret_s = _retention(seq(qr), seq(kr), seq(vr), seq(gr), state_ret[l], seq_s)
        hs = _merge(hs, y_s5.reshape(n_s, -1), y_moba.reshape(n_s, -1), y_ret.reshape(n_s, -1), gates, w, last)
        outs["ks"].append(km.reshape(bsz_s, seq_s, MOBA_HEADS, MOBA_HEAD_DIM))
        outs["vs"].append(vm.reshape(bsz_s, seq_s, MOBA_HEADS, MOBA_HEAD_DIM))
        outs["s5rs"].append(s5r.reshape(bsz_s, S5_GROUPS, S5_STATE))
        outs["s5is"].append(s5i.reshape(bsz_s, S5_GROUPS, S5_STATE))
        outs["rets"].append(ret_s)

    stack = lambda name: jnp.stack(outs[name])
    return (hp.reshape(x_prompt.shape), hs.reshape(x_sample.shape),
            stack("kp"), stack("vp"), stack("ks"), stack("vs"),
            stack("s5rp"), stack("s5ip"), stack("s5rs"), stack("s5is"), stack("retp"), stack("rets"))
```

```python
import functools
import math

import jax
import jax.numpy as jnp
from jax import lax
from jax.experimental import pallas as pl
from jax.experimental.pallas import tpu as pltpu

F32 = jnp.float32
BF16 = jnp.bfloat16

D_MODEL = 1024
PAGE_SIZE = 128
S5_WIDTH = 512
S5_GROUP = 16
S5_GROUPS = 32
S5_STATE = 64
S5_FLAT = S5_GROUPS * S5_STATE
MOBA_HEADS = 8
MOBA_HEAD_DIM = 64
MOBA_WIDTH = 512
MOBA_BLOCK = 256
MOBA_TOPK = 3
MOBA_QBLOCK = 128
RET_HEADS = 4
RET_DK = 64
RET_DV = 128
RET_QK_WIDTH = 256
RET_V_WIDTH = 512
RET_CHUNK = 128
D_FF = 4096
ROPE_THETA = 10000.0
NORM_EPS = 1e-6
NEG_INF = -1e30
IN_SPLITS = (S5_WIDTH, MOBA_WIDTH, MOBA_WIDTH, MOBA_WIDTH, RET_QK_WIDTH, RET_QK_WIDTH,
             RET_V_WIDTH, RET_V_WIDTH, D_MODEL, D_MODEL, D_MODEL)
IN_COLS = sum(IN_SPLITS)
(OFF_U, OFF_QM, OFF_KM, OFF_VM, OFF_QR, OFF_KR, OFF_VR, OFF_GR, OFF_GATES) = (
    0, 512, 1024, 1536, 2048, 2304, 2560, 3072, 3584)

LANES = 128
SUBLANES = 8
ROW_TILE = 256
S5_BUNDLE = 8
S5_LANE_GROUP = 512
SAMPLE_BLOCKS_PER_STEP = 2
VMEM_LIMIT = 56 * 1024 * 1024

NT_DIMS = (((1,), (1,)), ((), ()))
TN_DIMS = (((0,), (0,)), ((), ()))


def _const_spec(shape):
    nd = len(shape)
    return pl.BlockSpec(shape, lambda *_: (0,) * nd, pipeline_mode=pl.Buffered(1))


def _params(n_axes):
    return pltpu.CompilerParams(dimension_semantics=("arbitrary",) * n_axes,
                                vmem_limit_bytes=VMEM_LIMIT)


def _rmsnorm(x, g):
    return x * lax.rsqrt(jnp.mean(x * x, axis=-1, keepdims=True) + NORM_EPS) * g


def _mm(a, w):
    return jnp.dot(a.astype(BF16), w, preferred_element_type=F32)


def _rope(t, cos, sin_signed):
    n = t.shape[1]
    reps = n // LANES
    c = jnp.concatenate([cos] * reps, axis=1)
    s = jnp.concatenate([sin_signed] * reps, axis=1)
    lane = lax.broadcasted_iota(jnp.int32, t.shape, 1)
    first_half = (lane % MOBA_HEAD_DIM) < (MOBA_HEAD_DIM // 2)
    partner = jnp.where(first_half, pltpu.roll(t, n - MOBA_HEAD_DIM // 2, 1),
                        pltpu.roll(t, MOBA_HEAD_DIM // 2, 1))
    return t * c + partner * s


def _inproj_body(x_ref, g_ref, w_ref, cos_ref, sin_ref,
                 u_ref, qm_ref, km_ref, vm_ref, kb_ref, vb_ref, qr_ref, kr_ref, vr_ref, gr_ref,
                 gate_ref, ksum_ref):
    hb = _rmsnorm(x_ref[...], g_ref[...]).astype(BF16)
    cos = cos_ref[...]
    sin = sin_ref[...]

    def proj(lo, n):
        return jnp.dot(hb, w_ref[:, lo:lo + n], preferred_element_type=F32)

    u_ref[...] = proj(OFF_U, S5_WIDTH)
    qm_ref[...] = _rope(proj(OFF_QM, MOBA_WIDTH), cos, sin)
    km = _rope(proj(OFF_KM, MOBA_WIDTH), cos, sin)
    km_ref[...] = km
    kb_ref[...] = km.astype(BF16)
    ksum_ref[0] = jnp.sum(km, axis=0, keepdims=True)
    vm = proj(OFF_VM, MOBA_WIDTH)
    vm_ref[...] = vm
    vb_ref[...] = vm.astype(BF16)
    qr_ref[...] = _rope(proj(OFF_QR, RET_QK_WIDTH), cos, sin)
    kr_ref[...] = _rope(proj(OFF_KR, RET_QK_WIDTH), cos, sin) * (RET_DK ** -0.5)
    vr_ref[...] = proj(OFF_VR, RET_V_WIDTH)
    gr = proj(OFF_GR, RET_V_WIDTH)
    gr_ref[...] = gr * jax.nn.sigmoid(gr)
    for i in range(3):
        gate_ref[:, i * D_MODEL:(i + 1) * D_MODEL] = jax.nn.sigmoid(proj(OFF_GATES + i * D_MODEL, D_MODEL))


def _inproj(x, g, w, cos, sin):
    n = x.shape[0]
    nt = n // ROW_TILE
    row = lambda width: pl.BlockSpec((ROW_TILE, width), lambda i: (i, 0))
    widths = (S5_WIDTH, MOBA_WIDTH, MOBA_WIDTH, MOBA_WIDTH, MOBA_WIDTH, MOBA_WIDTH,
              RET_QK_WIDTH, RET_QK_WIDTH, RET_V_WIDTH, RET_V_WIDTH, 3 * D_MODEL)
    dtypes = (F32, F32, F32, F32, BF16, BF16, F32, F32, F32, F32, F32)
    out_shape = [jax.ShapeDtypeStruct((n, wd), dt) for wd, dt in zip(widths, dtypes)]
    out_shape.append(jax.ShapeDtypeStruct((nt, 1, MOBA_WIDTH), F32))
    out_specs = [row(wd) for wd in widths]
    out_specs.append(pl.BlockSpec((1, 1, MOBA_WIDTH), lambda i: (i, 0, 0)))
    return pl.pallas_call(
        _inproj_body,
        out_shape=out_shape,
        grid=(nt,),
        in_specs=[row(D_MODEL), _const_spec((1, D_MODEL)), _const_spec((D_MODEL, IN_COLS)),
                  row(LANES), row(LANES)],
        out_specs=out_specs,
        compiler_params=_params(1),
        name="inproj",
    )(x, g, w, cos, sin)


def _gelu_tanh(x):
    cdf = 0.5 * (1.0 + jnp.tanh(math.sqrt(2.0 / math.pi) * (x + 0.044715 * (x * x * x))))
    return x * cdf


def _s5_body(u_ref, x0r_ref, x0i_ref, lamr_ref, lami_ref, ljr_ref, lji_ref, wbr_ref, wbi_ref,
             wcr_ref, wci_ref, d_ref, wg_ref,
             y_ref, xlr_ref, xli_ref,
             xr_s, xi_s, cr_s, ci_s, carr_s, cari_s, *, steps, chained):
    n_bundles = S5_GROUPS // S5_BUNDLE
    in_w = S5_BUNDLE * S5_GROUP
    st_w = S5_BUNDLE * S5_STATE
    u = u_ref[...]
    ub = u.astype(BF16)
    for j in range(n_bundles):
        uj = ub[:, j * in_w:(j + 1) * in_w]
        xr_s[:, j * st_w:(j + 1) * st_w] = jnp.dot(uj, wbr_ref[j], preferred_element_type=F32)
        xi_s[:, j * st_w:(j + 1) * st_w] = jnp.dot(uj, wbi_ref[j], preferred_element_type=F32)

    def scan(init_r, init_i, lo, store):
        lanes = slice(lo, lo + S5_LANE_GROUP)
        lr = lamr_ref[:, lanes]
        li = lami_ref[:, lanes]

        def step(t, carry):
            xr, xi = carry
            rows = pl.ds(pl.multiple_of(t * SUBLANES, SUBLANES), SUBLANES)
            nr = lr * xr - li * xi + xr_s[rows, lanes]
            ni = lr * xi + li * xr + xi_s[rows, lanes]
            if store:
                xr_s[rows, lanes] = nr
                xi_s[rows, lanes] = ni
            return nr, ni

        return lax.fori_loop(0, steps, step, (init_r, init_i), unroll=4)

    if chained:
        @pl.when(pl.program_id(1) == 0)
        def _():
            carr_s[...] = x0r_ref[...]
            cari_s[...] = x0i_ref[...]

    zeros = jnp.zeros((SUBLANES, S5_LANE_GROUP), F32)
    for lo in range(0, S5_FLAT, S5_LANE_GROUP):
        lanes = slice(lo, lo + S5_LANE_GROUP)
        if chained:
            end_r, end_i = scan(zeros, zeros, lo, store=False)
            c_r = carr_s[:, lanes]
            c_i = cari_s[:, lanes]
            ljr = ljr_ref[:, lanes]
            lji = lji_ref[:, lanes]
            for s in range(SUBLANES):
                cr_s[s:s + 1, lanes] = c_r
                ci_s[s:s + 1, lanes] = c_i
                n_r = ljr * c_r - lji * c_i + end_r[s:s + 1, :]
                n_i = ljr * c_i + lji * c_r + end_i[s:s + 1, :]
                c_r, c_i = n_r, n_i
            carr_s[:, lanes] = c_r
            cari_s[:, lanes] = c_i
            scan(cr_s[:, lanes], ci_s[:, lanes], lo, store=True)
        else:
            f_r, f_i = scan(x0r_ref[:, lanes], x0i_ref[:, lanes], lo, store=True)
            xlr_ref[:, lanes] = f_r
            xli_ref[:, lanes] = f_i
    if chained:
        xlr_ref[...] = carr_s[...]
        xli_ref[...] = cari_s[...]

    ys = []
    for j in range(n_bundles):
        sl = slice(j * st_w, (j + 1) * st_w)
        ys.append(_mm(xr_s[:, sl], wcr_ref[j]) + _mm(xi_s[:, sl], wci_ref[j]))
    y = jnp.concatenate(ys, axis=1) + d_ref[...] * u
    y = _gelu_tanh(y)
    y_ref[...] = y * jax.nn.sigmoid(_mm(y, wg_ref[...]))


def _s5_mixer(u, x0r, x0i, prm, *, steps, chained):
    n_groups, rows, _ = u.shape
    t = SUBLANES * steps
    n_chunks = rows // t

    def swap_row_order(a, inner, outer):
        return a.reshape(n_groups, n_chunks, inner, outer, -1).transpose(0, 1, 3, 2, 4).reshape(a.shape)

    u = swap_row_order(u, SUBLANES, steps)
    r0 = x0r.shape[1]
    chunk = pl.BlockSpec((None, t, S5_WIDTH), lambda b, c: (b, c, 0))
    state = pl.BlockSpec((None, r0, S5_FLAT), lambda b, c: (b, 0, 0))
    n_bundles = S5_GROUPS // S5_BUNDLE
    in_w = S5_BUNDLE * S5_GROUP
    st_w = S5_BUNDLE * S5_STATE
    body = functools.partial(_s5_body, steps=steps, chained=chained)
    y, xl_r, xl_i = pl.pallas_call(
        body,
        out_shape=[jax.ShapeDtypeStruct(u.shape, F32),
                   jax.ShapeDtypeStruct(x0r.shape, F32), jax.ShapeDtypeStruct(x0r.shape, F32)],
        grid=(n_groups, n_chunks),
        in_specs=[chunk, state, state,
                  _const_spec((SUBLANES, S5_FLAT)), _const_spec((SUBLANES, S5_FLAT)),
                  _const_spec((1, S5_FLAT)), _const_spec((1, S5_FLAT)),
                  _const_spec((n_bundles, in_w, st_w)), _const_spec((n_bundles, in_w, st_w)),
                  _const_spec((n_bundles, st_w, in_w)), _const_spec((n_bundles, st_w, in_w)),
                  _const_spec((1, S5_WIDTH)), _const_spec((S5_WIDTH, S5_WIDTH))],
        out_specs=[chunk, state, state],
        scratch_shapes=[pltpu.VMEM((t, S5_FLAT), F32), pltpu.VMEM((t, S5_FLAT), F32),
                        pltpu.VMEM((SUBLANES, S5_FLAT), F32), pltpu.VMEM((SUBLANES, S5_FLAT), F32),
                        pltpu.VMEM((1, S5_FLAT), F32), pltpu.VMEM((1, S5_FLAT), F32)],
        compiler_params=_params(2),
        name="s5_chained" if chained else "s5_independent",
    )(u, x0r, x0i, prm["lam_r8"], prm["lam_i8"], prm["lj_r"][steps], prm["lj_i"][steps],
      prm["wb_r"], prm["wb_i"], prm["wc_r"], prm["wc_i"], prm["d"], prm["w_glu"])
    return swap_row_order(y, steps, SUBLANES), xl_r, xl_i


def _s5_prepare(lam_re, lam_im, log_dt, b_re, b_im, c_re, c_im, d_skip, w_glu, step_counts):
    a = jnp.minimum(lam_re.astype(F32), -1e-4)
    b = lam_im.astype(F32)
    dt = jnp.exp(log_dt.astype(F32))[:, None]

    def lam_bar_power(n):
        mag = jnp.exp(a * dt * n)
        return mag * jnp.cos(b * dt * n), mag * jnp.sin(b * dt * n)

    e_r, e_i = lam_bar_power(1.0)
    den = a * a + b * b
    cf_r = ((e_r - 1.0) * a + e_i * b) / den
    cf_i = (e_i * a - (e_r - 1.0) * b) / den
    bb_r = cf_r[..., None] * b_re.astype(F32) - cf_i[..., None] * b_im.astype(F32)
    bb_i = cf_r[..., None] * b_im.astype(F32) + cf_i[..., None] * b_re.astype(F32)
    n_bundles = S5_GROUPS // S5_BUNDLE
    eye = jnp.eye(S5_BUNDLE, dtype=F32)

    def expand_b(t):
        t = t.reshape(n_bundles, S5_BUNDLE, S5_STATE, S5_GROUP)
        w = jnp.einsum("jgpc,gh->jgchp", t, eye)
        return w.reshape(n_bundles, S5_BUNDLE * S5_GROUP, S5_BUNDLE * S5_STATE).astype(BF16)

    def expand_c(t):
        t = t.reshape(n_bundles, S5_BUNDLE, S5_GROUP, S5_STATE)
        w = jnp.einsum("jgcp,gh->jgphc", t, eye)
        return w.reshape(n_bundles, S5_BUNDLE * S5_STATE, S5_BUNDLE * S5_GROUP).astype(BF16)

    flat = lambda t: t.reshape(1, S5_FLAT)
    lj_r, lj_i = {}, {}
    for steps in step_counts:
        p_r, p_i = lam_bar_power(float(steps))
        lj_r[steps] = flat(p_r)
        lj_i[steps] = flat(p_i)
    return {
        "lam_r8": jnp.broadcast_to(flat(e_r), (SUBLANES, S5_FLAT)),
        "lam_i8": jnp.broadcast_to(flat(e_i), (SUBLANES, S5_FLAT)),
        "lj_r": lj_r, "lj_i": lj_i,
        "wb_r": expand_b(bb_r), "wb_i": expand_b(bb_i),
        "wc_r": expand_c(c_re.astype(F32)), "wc_i": expand_c(-c_im.astype(F32)),
        "d": d_skip.astype(F32).reshape(1, S5_WIDTH),
        "w_glu": w_glu.astype(BF16),
    }


def _retention_body(q_ref, k_ref, v_ref, g_ref, s0_ref, dm_ref, qd_ref, kd_ref, cd_ref,
                    y_ref, sout_ref, s_s):
    @pl.when(pl.program_id(1) == 0)
    def _():
        s_s[...] = s0_ref[...]

    for h in range(RET_HEADS):
        qk = slice(h * RET_DK, (h + 1) * RET_DK)
        vv = slice(h * RET_DV, (h + 1) * RET_DV)
        q = q_ref[:, qk].astype(BF16)
        k = k_ref[:, qk]
        v = v_ref[:, vv].astype(BF16)
        s = s_s[h]
        inner = lax.dot_general(q, k.astype(BF16), NT_DIMS, preferred_element_type=F32) * dm_ref[h]
        o = (jnp.dot(inner.astype(BF16), v, preferred_element_type=F32)
             + jnp.dot(q, s.astype(BF16), preferred_element_type=F32) * qd_ref[h])
        k_dec = (k * kd_ref[h]).astype(BF16)
        s_s[h] = s * cd_ref[h] + lax.dot_general(k_dec, v, TN_DIMS, preferred_element_type=F32)
        mu = jnp.mean(o, axis=-1, keepdims=True)
        var = jnp.mean(jnp.square(o - mu), axis=-1, keepdims=True)
        y_ref[:, vv] = g_ref[:, vv] * ((o - mu) * lax.rsqrt(var + NORM_EPS))
    sout_ref[...] = s_s[...]


def _retention(q, k, v, g, s0, chunk):
    bsz, seq_len, _ = q.shape
    n_chunks = seq_len // chunk
    log_g = jnp.log(1.0 - 2.0 ** (-5.0 - jnp.arange(RET_HEADS, dtype=F32)))
    idx = jnp.arange(chunk, dtype=F32)
    diff = idx[:, None] - idx[None, :]
    decay_mask = jnp.where(diff >= 0, jnp.exp(log_g[:, None, None] * jnp.maximum(diff, 0.0)), 0.0)
    q_decay = jnp.exp(log_g[:, None] * (idx + 1.0))[:, :, None]
    k_decay = jnp.exp(log_g[:, None] * (chunk - 1.0 - idx))[:, :, None]
    chunk_decay = jnp.broadcast_to(jnp.exp(log_g * chunk)[:, None, None], (RET_HEADS, 1, RET_DV))
    rows = lambda width: pl.BlockSpec((None, chunk, width), lambda b, c: (b, c, 0))
    state = pl.BlockSpec((None, RET_HEADS, RET_DK, RET_DV), lambda b, c: (b, 0, 0, 0))
    return pl.pallas_call(
        _retention_body,
        out_shape=[jax.ShapeDtypeStruct(v.shape, F32), jax.ShapeDtypeStruct(s0.shape, F32)],
        grid=(bsz, n_chunks),
        in_specs=[rows(RET_QK_WIDTH), rows(RET_QK_WIDTH), rows(RET_V_WIDTH), rows(RET_V_WIDTH), state,
                  _const_spec((RET_HEADS, chunk, chunk)), _const_spec((RET_HEADS, chunk, 1)),
                  _const_spec((RET_HEADS, chunk, 1)), _const_spec((RET_HEADS, 1, RET_DV))],
        out_specs=[rows(RET_V_WIDTH), state],
        scratch_shapes=[pltpu.VMEM((RET_HEADS, RET_DK, RET_DV), F32)],
        compiler_params=_params(2),
        name=f"retention_c{chunk}",
    )(q, k, v, g, s0.astype(F32), decay_mask, q_decay, k_decay, chunk_decay)


def _topk_rank(gate, axis, n):
    idx = lax.broadcasted_iota(jnp.int32, gate.shape, axis)
    rank = jnp.zeros(gate.shape, jnp.int32)
    for other in range(n):
        g_o = gate[:, other:other + 1] if axis == 1 else gate[other:other + 1, :]
        beats = (g_o > gate) | ((g_o == gate) & (other < idx))
        rank = rank + beats.astype(jnp.int32)
    return rank


def _moba_prompt_body(q_ref, k_ref, vt_ref, ksum_ref, o_ref, w_s, sel_s, s_s, p_s, acc_s, *, n_blocks):
    qb = pl.program_id(1)
    own = (qb * MOBA_QBLOCK) // MOBA_BLOCK
    q_off = (qb * MOBA_QBLOCK) % MOBA_BLOCK
    key_i = lax.broadcasted_iota(jnp.int32, (MOBA_BLOCK, MOBA_QBLOCK), 0)
    q_i = lax.broadcasted_iota(jnp.int32, (MOBA_BLOCK, MOBA_QBLOCK), 1)
    causal = key_i <= q_off + q_i
    blk = lax.broadcasted_iota(jnp.int32, (n_blocks, MOBA_QBLOCK), 0)
    scale = MOBA_HEAD_DIM ** -0.5
    q_t = q_ref[...].T
    zero_half = jnp.zeros((MOBA_HEAD_DIM, MOBA_QBLOCK), F32)
    for h in range(MOBA_HEADS):
        hd = slice(h * MOBA_HEAD_DIM, (h + 1) * MOBA_HEAD_DIM)
        q_h = q_t[hd, :]
        kmean = ksum_ref[:, hd] * (1.0 / MOBA_BLOCK)
        gate = jnp.dot(kmean, q_h, precision=lax.Precision.HIGHEST, preferred_element_type=F32)
        gate = jnp.where(blk < own, gate, NEG_INF)
        sel_s[h] = ((_topk_rank(gate, 0, n_blocks) < MOBA_TOPK) & (blk < own)).astype(F32)
        halves = [q_h * scale, zero_half] if h % 2 == 0 else [zero_half, q_h * scale]
        w_s[h] = jnp.concatenate(halves, axis=0).astype(BF16)
    acc_s[...] = jnp.zeros(acc_s.shape, F32)

    def block_update(n, allowed_of, m, l):
        rows = pl.ds(pl.multiple_of(n * MOBA_BLOCK, MOBA_BLOCK), MOBA_BLOCK)
        for h in range(MOBA_HEADS):
            pair = slice((h // 2) * LANES, (h // 2 + 1) * LANES)
            s_s[h] = jnp.dot(k_ref[rows, pair], w_s[h], preferred_element_type=F32)
        m_out, l_out, alphas = [], [], []
        for h in range(MOBA_HEADS):
            s = jnp.where(allowed_of(h), s_s[h], NEG_INF)
            m_new = jnp.maximum(m[h], jnp.max(s, axis=0, keepdims=True))
            alpha = jnp.exp(m[h] - m_new)
            p = jnp.exp(s - m_new)
            p_s[h] = p.astype(BF16)
            l_out.append(alpha * l[h] + jnp.sum(p, axis=0, keepdims=True))
            m_out.append(m_new)
            alphas.append(alpha)
        for h in range(MOBA_HEADS):
            hd = slice(h * MOBA_HEAD_DIM, (h + 1) * MOBA_HEAD_DIM)
            acc_s[hd, :] = alphas[h] * acc_s[hd, :] + jnp.dot(vt_ref[n, hd, :], p_s[h],
                                                              preferred_element_type=F32)
        return tuple(m_out), tuple(l_out)

    m0 = tuple(jnp.full((1, MOBA_QBLOCK), NEG_INF, F32) for _ in range(MOBA_HEADS))
    l0 = tuple(jnp.zeros((1, MOBA_QBLOCK), F32) for _ in range(MOBA_HEADS))
    m, l = block_update(own, lambda h: causal, m0, l0)

    def past_block(n, carry):
        def allowed_of(h):
            return jnp.broadcast_to(sel_s[h, pl.ds(n, 1), :] > 0.5, (MOBA_BLOCK, MOBA_QBLOCK))
        return block_update(n, allowed_of, *carry)

    m, l = lax.fori_loop(0, own, past_block, (m, l))
    outs = []
    for h in range(MOBA_HEADS):
        hd = slice(h * MOBA_HEAD_DIM, (h + 1) * MOBA_HEAD_DIM)
        outs.append(acc_s[hd, :] / l[h])
    o_ref[...] = jnp.concatenate(outs, axis=0).T


def _moba_prompt(q, kb, vt, ksum):
    bsz, seq_len, _ = q.shape
    n_blocks = seq_len // MOBA_BLOCK
    qblock = pl.BlockSpec((None, MOBA_QBLOCK, MOBA_WIDTH), lambda b, i: (b, i, 0))
    return pl.pallas_call(
        functools.partial(_moba_prompt_body, n_blocks=n_blocks),
        out_shape=jax.ShapeDtypeStruct(q.shape, F32),
        grid=(bsz, seq_len // MOBA_QBLOCK),
        in_specs=[qblock,
                  pl.BlockSpec((None, seq_len, MOBA_WIDTH), lambda b, i: (b, 0, 0)),
                  pl.BlockSpec((None, n_blocks, MOBA_WIDTH, MOBA_BLOCK), lambda b, i: (b, 0, 0, 0)),
                  pl.BlockSpec((None, n_blocks, MOBA_WIDTH), lambda b, i: (b, 0, 0))],
        out_specs=qblock,
        scratch_shapes=[pltpu.VMEM((MOBA_HEADS, LANES, MOBA_QBLOCK), BF16),
                        pltpu.VMEM((MOBA_HEADS, n_blocks, MOBA_QBLOCK), F32),
                        pltpu.VMEM((MOBA_HEADS, MOBA_BLOCK, MOBA_QBLOCK), F32),
                        pltpu.VMEM((MOBA_HEADS, MOBA_BLOCK, MOBA_QBLOCK), BF16),
                        pltpu.VMEM((MOBA_WIDTH, MOBA_QBLOCK), F32)],
        compiler_params=_params(2),
        name="moba_prompt",
    )(q, kb, vt, ksum)


def _moba_sample_body(pt_ref, q_ref, kn_ref, vn_ref, *rest, n_pages, dec_seq):
    del pt_ref
    pages_per_block = MOBA_BLOCK // PAGE_SIZE
    pps = SAMPLE_BLOCKS_PER_STEP * pages_per_block
    ck_refs = rest[:pps]
    cv_refs = rest[pps:2 * pps]
    o_ref, qbd_s, kmt_s, m_s, l_s, acc_s = rest[2 * pps:]
    step = pl.program_id(1)
    n_steps = n_pages // pps
    n_blocks = n_pages // pages_per_block
    n_cols = MOBA_HEADS * dec_seq
    scale = MOBA_HEAD_DIM ** -0.5
    r_i = lax.broadcasted_iota(jnp.int32, (n_cols, MOBA_WIDTH), 0)
    c_i = lax.broadcasted_iota(jnp.int32, (n_cols, MOBA_WIDTH), 1)
    head_diag = (r_i // dec_seq) == (c_i // MOBA_HEAD_DIM)

    @pl.when(step == 0)
    def _():
        q_rep = jnp.concatenate([q_ref[...]] * MOBA_HEADS, axis=0)
        qbd_s[...] = jnp.where(head_diag, q_rep, 0.0)
        m_s[...] = jnp.full(m_s.shape, NEG_INF, F32)
        l_s[...] = jnp.zeros(l_s.shape, F32)

    qs = (qbd_s[...] * scale).astype(BF16)
    blk_lane = lax.broadcasted_iota(jnp.int32, (n_cols, n_blocks), 1)
    kmt_lane = lax.broadcasted_iota(jnp.int32, (MOBA_WIDTH, n_blocks), 1)
    for j in range(SAMPLE_BLOCKS_PER_STEP):
        block = step * SAMPLE_BLOCKS_PER_STEP + j
        kt = jnp.concatenate([ck_refs[j * pages_per_block + i][...].reshape(MOBA_WIDTH, PAGE_SIZE)
                              for i in range(pages_per_block)], axis=1)
        vt = jnp.concatenate([cv_refs[j * pages_per_block + i][...].reshape(MOBA_WIDTH, PAGE_SIZE)
                              for i in range(pages_per_block)], axis=1)
        ksum = jnp.sum(kt, axis=1, keepdims=True)
        kmt_s[...] = jnp.where(kmt_lane == block, ksum * (1.0 / MOBA_BLOCK), kmt_s[...])
        s = jnp.dot(qs, kt.astype(BF16), preferred_element_type=F32)
        m_n = jnp.max(s, axis=1, keepdims=True)
        p = jnp.exp(s - m_n)
        l_n = jnp.sum(p, axis=1, keepdims=True)
        acc_s[block] = lax.dot_general(p.astype(BF16), vt.astype(BF16), NT_DIMS,
                                       preferred_element_type=F32)
        m_s[...] = jnp.where(blk_lane == block, m_n, m_s[...])
        l_s[...] = jnp.where(blk_lane == block, l_n, l_s[...])

    @pl.when(step == n_steps - 1)
    def _():
        gate = jnp.dot(qbd_s[...], kmt_s[...], precision=lax.Precision.HIGHEST,
                       preferred_element_type=F32)
        sel = _topk_rank(gate, 1, n_blocks) < MOBA_TOPK
        s_own = lax.dot_general(qs, kn_ref[...].astype(BF16), NT_DIMS, preferred_element_type=F32)
        o_row = lax.broadcasted_iota(jnp.int32, s_own.shape, 0)
        o_lane = lax.broadcasted_iota(jnp.int32, s_own.shape, 1)
        s_own = jnp.where(o_lane <= (o_row % dec_seq), s_own, NEG_INF)
        m_own = jnp.max(s_own, axis=1, keepdims=True)
        p_own = jnp.exp(s_own - m_own)
        l_own = jnp.sum(p_own, axis=1, keepdims=True)
        acc_own = jnp.dot(p_own.astype(BF16), vn_ref[...].astype(BF16), preferred_element_type=F32)
        m_all = m_s[...]
        top = jnp.maximum(m_own, jnp.max(jnp.where(sel, m_all, NEG_INF), axis=1, keepdims=True))
        w = jnp.exp(jnp.where(sel, m_all - top, NEG_INF))
        w_own = jnp.exp(m_own - top)
        denom = w_own * l_own + jnp.sum(w * l_s[...], axis=1, keepdims=True)
        out = w_own * acc_own
        for n in range(n_blocks):
            out = out + w[:, n:n + 1] * acc_s[n]
        out = jnp.where(head_diag, out / denom, 0.0)
        res = out[0:dec_seq]
        for h in range(1, MOBA_HEADS):
            res = res + out[h * dec_seq:(h + 1) * dec_seq]
        o_ref[...] = res


def _moba_sample(layer, page_table, q, k_new, v_new, cache_kt, cache_vt):
    bsz, dec_seq, _ = q.shape
    n_pages = page_table.shape[1]
    pages_per_block = MOBA_BLOCK // PAGE_SIZE
    pps = SAMPLE_BLOCKS_PER_STEP * pages_per_block
    n_steps = n_pages // pps
    n_blocks = n_pages // pages_per_block
    n_cols = MOBA_HEADS * dec_seq
    tok = pl.BlockSpec((None, dec_seq, MOBA_WIDTH), lambda b, s, pt: (b, 0, 0))

    def page_spec(i):
        return pl.BlockSpec((None, None, MOBA_HEADS, MOBA_HEAD_DIM, PAGE_SIZE),
                            lambda b, s, pt: (layer, pt[b, s * pps + i], 0, 0, 0))

    grid_spec = pltpu.PrefetchScalarGridSpec(
        num_scalar_prefetch=1,
        grid=(bsz, n_steps),
        in_specs=[tok, tok, tok] + [page_spec(i) for i in range(pps)] * 2,
        out_specs=tok,
        scratch_shapes=[pltpu.VMEM((n_cols, MOBA_WIDTH), F32),
                        pltpu.VMEM((MOBA_WIDTH, n_blocks), F32),
                        pltpu.VMEM((n_cols, n_blocks), F32),
                        pltpu.VMEM((n_cols, n_blocks), F32),
                        pltpu.VMEM((n_blocks, n_cols, MOBA_WIDTH), F32)],
    )
    return pl.pallas_call(
        functools.partial(_moba_sample_body, n_pages=n_pages, dec_seq=dec_seq),
        out_shape=jax.ShapeDtypeStruct(q.shape, F32),
        grid_spec=grid_spec,
        compiler_params=_params(2),
        name="moba_sample",
    )(page_table, q, k_new, v_new, *([cache_kt] * pps), *([cache_vt] * pps))


def _merge_body(x_ref, ys_ref, ym_ref, yr_ref, gate_ref, wbs_ref, wbm_ref, wbr_ref, wo_ref, g2_ref,
                wu_ref, wd_ref, gf_ref, o_ref, *, final):
    merged = (gate_ref[:, 0:D_MODEL] * _mm(ys_ref[...], wbs_ref[...])
              + gate_ref[:, D_MODEL:2 * D_MODEL] * _mm(ym_ref[...], wbm_ref[...])
              + gate_ref[:, 2 * D_MODEL:3 * D_MODEL] * _mm(yr_ref[...], wbr_ref[...]))
    x = x_ref[...] + _mm(merged, wo_ref[...])
    up = _mm(_rmsnorm(x, g2_ref[...]), wu_ref[...])
    x = x + _mm(jnp.square(jnp.maximum(up, 0.0)), wd_ref[...])
    if final:
        x = _rmsnorm(x, gf_ref[...])
    o_ref[...] = x


def _merge(x, y_s5, y_moba, y_ret, gates, w, final):
    n = x.shape[0]
    row = lambda width: pl.BlockSpec((ROW_TILE, width), lambda i: (i, 0))
    return pl.pallas_call(
        functools.partial(_merge_body, final=final),
        out_shape=jax.ShapeDtypeStruct(x.shape, F32),
        grid=(n // ROW_TILE,),
        in_specs=[row(D_MODEL), row(S5_WIDTH), row(MOBA_WIDTH), row(RET_V_WIDTH), row(3 * D_MODEL),
                  _const_spec((S5_WIDTH, D_MODEL)), _const_spec((MOBA_WIDTH, D_MODEL)),
                  _const_spec((RET_V_WIDTH, D_MODEL)), _const_spec((D_MODEL, D_MODEL)),
                  _const_spec((1, D_MODEL)), _const_spec((D_MODEL, D_FF)), _const_spec((D_FF, D_MODEL)),
                  _const_spec((1, D_MODEL))],
        out_specs=row(D_MODEL),
        compiler_params=_params(1),
        name="merge_mlp",
    )(x, y_s5, y_moba, y_ret, gates, w["w_br_s5"], w["w_br_moba"], w["w_br_ret"], w["w_out"],
      w["norm2_g"], w["w_mlp_up"], w["w_mlp_down"], w["final_g"])


def _rope_tables(pos):
    half = MOBA_HEAD_DIM // 2
    freqs = ROPE_THETA ** (-jnp.arange(half, dtype=F32) / half)
    ang = pos.astype(F32)[:, None] * freqs[None, :]
    cos = jnp.cos(ang)
    sin = jnp.sin(ang)
    reps = LANES // MOBA_HEAD_DIM
    return (jnp.concatenate([cos, cos] * reps, axis=1), jnp.concatenate([-sin, sin] * reps, axis=1))


def kernel(x_prompt, x_sample, cache_k, cache_v, state_s5_re, state_s5_im, state_ret, page_table,
           norm1_g, w_in, s5_lambda_re, s5_lambda_im, s5_log_dt, s5_b_re, s5_b_im, s5_c_re, s5_c_im,
           s5_d, s5_w_glu, w_br_s5, w_br_moba, w_br_ret, w_out, norm2_g, w_mlp_up, w_mlp_down, final_g):
    bsz_p, seq_p, _ = x_prompt.shape
    bsz_s, seq_s, _ = x_sample.shape
    depth = w_in.shape[0]
    n_pages = page_table.shape[1]
    past_len = n_pages * PAGE_SIZE
    n_p = bsz_p * seq_p
    n_s = bsz_s * seq_s
    prompt_steps = ROW_TILE // SUBLANES
    assert seq_p % ROW_TILE == 0 and n_s % ROW_TILE == 0 and seq_p % RET_CHUNK == 0
    assert n_pages % (SAMPLE_BLOCKS_PER_STEP * MOBA_BLOCK // PAGE_SIZE) == 0
    assert seq_s == SUBLANES and bsz_s % SUBLANES == 0 and seq_s % RET_CHUNK != 0

    cos_p, sin_p = _rope_tables(jnp.tile(jnp.arange(seq_p, dtype=jnp.int32), bsz_p))
    cos_s, sin_s = _rope_tables(jnp.tile(past_len + jnp.arange(seq_s, dtype=jnp.int32), bsz_s))
    zeros_s5 = jnp.zeros((bsz_p, 1, S5_FLAT), F32)
    zeros_ret = jnp.zeros((bsz_p, RET_HEADS, RET_DK, RET_DV), F32)
    cache_kt = cache_k.transpose(0, 1, 3, 4, 2)
    cache_vt = cache_v.transpose(0, 1, 3, 4, 2)

    hp = x_prompt.reshape(n_p, D_MODEL)
    hs = x_sample.reshape(n_s, D_MODEL)
    outs = {name: [] for name in ("kp", "vp", "ks", "vs", "s5rp", "s5ip", "s5rs", "s5is", "retp", "rets")}
    for l in range(depth):
        last = l == depth - 1
        w = {"w_br_s5": w_br_s5[l].astype(BF16), "w_br_moba": w_br_moba[l].astype(BF16),
             "w_br_ret": w_br_ret[l].astype(BF16), "w_out": w_out[l].astype(BF16),
             "norm2_g": norm2_g[l].reshape(1, D_MODEL), "w_mlp_up": w_mlp_up[l].astype(BF16),
             "w_mlp_down": w_mlp_down[l].astype(BF16), "final_g": final_g.reshape(1, D_MODEL)}
        g1 = norm1_g[l].reshape(1, D_MODEL)
        w_in_l = w_in[l].astype(BF16)
        s5p = _s5_prepare(s5_lambda_re[l], s5_lambda_im[l], s5_log_dt[l], s5_b_re[l], s5_b_im[l],
                          s5_c_re[l], s5_c_im[l], s5_d[l], s5_w_glu[l], (prompt_steps, seq_s))

        (u, qm, km, vm, kb, vb, qr, kr, vr, gr, gates, ksum) = _inproj(hp, g1, w_in_l, cos_p, sin_p)
        seq = lambda t: t.reshape(bsz_p, seq_p, t.shape[-1])
        y_s5, s5r, s5i = _s5_mixer(seq(u), zeros_s5, zeros_s5, s5p, steps=prompt_steps, chained=True)
        n_blk = seq_p // MOBA_BLOCK
        vt = vb.reshape(bsz_p, n_blk, MOBA_BLOCK, MOBA_WIDTH).transpose(0, 1, 3, 2)
        y_moba = _moba_prompt(seq(qm), seq(kb), vt, ksum.reshape(bsz_p, n_blk, MOBA_WIDTH))
        y_ret, ret_s = _retention(seq(qr), seq(kr), seq(vr), seq(gr), zeros_ret, RET_CHUNK)
        hp = _merge(hp, y_s5.reshape(n_p, -1), y_moba.reshape(n_p, -1), y_ret.reshape(n_p, -1), gates, w, last)
        outs["kp"].append(km.reshape(bsz_p, seq_p, MOBA_HEADS, MOBA_HEAD_DIM))
        outs["vp"].append(vm.reshape(bsz_p, seq_p, MOBA_HEADS, MOBA_HEAD_DIM))
        outs["s5rp"].append(s5r.reshape(bsz_p, S5_GROUPS, S5_STATE))
        outs["s5ip"].append(s5i.reshape(bsz_p, S5_GROUPS, S5_STATE))
        outs["retp"].append(ret_s)

        (u, qm, km, vm, kb, vb, qr, kr, vr, gr, gates, ksum) = _inproj(hs, g1, w_in_l, cos_s, sin_s)
        grp = lambda t: t.reshape(bsz_s // SUBLANES, SUBLANES * seq_s, t.shape[-1])
        st = lambda t: t.astype(F32).reshape(bsz_s // SUBLANES, SUBLANES, S5_FLAT)
        y_s5, s5r, s5i = _s5_mixer(grp(u), st(state_s5_re[l]), st(state_s5_im[l]), s5p,
                                   steps=seq_s, chained=False)
        seq = lambda t: t.reshape(bsz_s, seq_s, t.shape[-1])
        y_moba = _moba_sample(l, page_table, seq(qm), seq(km), seq(vm), cache_kt, cache_vt)
        y_ret, ret_s = _retention(seq(qr), seq(kr), seq(vr), seq(gr), state_ret[l], seq_s)
        hs = _merge(hs, y_s5.reshape(n_s, -1), y_moba.reshape(n_s, -1), y_ret.reshape(n_s, -1), gates, w, last)
        outs["ks"].append(km.reshape(bsz_s, seq_s, MOBA_HEADS, MOBA_HEAD_DIM))
        outs["vs"].append(vm.reshape(bsz_s, seq_s, MOBA_HEADS, MOBA_HEAD_DIM))
        outs["s5rs"].append(s5r.reshape(bsz_s, S5_GROUPS, S5_STATE))
        outs["s5is"].append(s5i.reshape(bsz_s, S5_GROUPS, S5_STATE))
        outs["rets"].append(ret_s)

    stack = lambda name: jnp.stack(outs[name])
    return (hp.reshape(x_prompt.shape), hs.reshape(x_sample.shape),
            stack("kp"), stack("vp"), stack("ks"), stack("vs"),
            stack("s5rp"), stack("s5ip"), stack("s5rs"), stack("s5is"), stack("retp"), stack("rets"))
```

```python
import functools
import math

import jax
import jax.numpy as jnp
from jax import lax
from jax.experimental import pallas as pl
from jax.experimental.pallas import tpu as pltpu

F32 = jnp.float32
BF16 = jnp.bfloat16

D_MODEL = 1024
PAGE_SIZE = 128
S5_WIDTH = 512
S5_GROUP = 16
S5_GROUPS = 32
S5_STATE = 64
S5_FLAT = S5_GROUPS * S5_STATE
MOBA_HEADS = 8
MOBA_HEAD_DIM = 64
MOBA_WIDTH = 512
MOBA_BLOCK = 256
MOBA_TOPK = 3
MOBA_QBLOCK = 128
RET_HEADS = 4
RET_DK = 64
RET_DV = 128
RET_QK_WIDTH = 256
RET_V_WIDTH = 512
RET_CHUNK = 128
D_FF = 4096
ROPE_THETA = 10000.0
NORM_EPS = 1e-6
NEG_INF = -1e30
IN_SPLITS = (S5_WIDTH, MOBA_WIDTH, MOBA_WIDTH, MOBA_WIDTH, RET_QK_WIDTH, RET_QK_WIDTH,
             RET_V_WIDTH, RET_V_WIDTH, D_MODEL, D_MODEL, D_MODEL)
IN_COLS = sum(IN_SPLITS)
(OFF_U, OFF_QM, OFF_KM, OFF_VM, OFF_QR, OFF_KR, OFF_VR, OFF_GR, OFF_GATES) = (
    0, 512, 1024, 1536, 2048, 2304, 2560, 3072, 3584)

LANES = 128
SUBLANES = 8
ROW_TILE = 256
S5_BUNDLE = 8
S5_LANE_GROUP = 512
SAMPLE_BLOCKS_PER_STEP = 4
VMEM_LIMIT = 56 * 1024 * 1024

NT_DIMS = (((1,), (1,)), ((), ()))
TN_DIMS = (((0,), (0,)), ((), ()))


def _const_spec(shape):
    nd = len(shape)
    return pl.BlockSpec(shape, lambda *_: (0,) * nd, pipeline_mode=pl.Buffered(1))


def _params(n_axes):
    return pltpu.CompilerParams(dimension_semantics=("arbitrary",) * n_axes,
                                vmem_limit_bytes=VMEM_LIMIT)


def _rmsnorm(x, g):
    return x * lax.rsqrt(jnp.mean(x * x, axis=-1, keepdims=True) + NORM_EPS) * g


def _mm(a, w):
    return jnp.dot(a.astype(BF16), w, preferred_element_type=F32)


def _rope(t, cos, sin_signed):
    n = t.shape[1]
    reps = n // LANES
    c = jnp.concatenate([cos] * reps, axis=1)
    s = jnp.concatenate([sin_signed] * reps, axis=1)
    lane = lax.broadcasted_iota(jnp.int32, t.shape, 1)
    first_half = (lane % MOBA_HEAD_DIM) < (MOBA_HEAD_DIM // 2)
    partner = jnp.where(first_half, pltpu.roll(t, n - MOBA_HEAD_DIM // 2, 1),
                        pltpu.roll(t, MOBA_HEAD_DIM // 2, 1))
    return t * c + partner * s


def _inproj_body(x_ref, g_ref, w_ref, cos_ref, sin_ref,
                 u_ref, qm_ref, km_ref, vm_ref, kb_ref, vb_ref, qr_ref, kr_ref, vr_ref, gr_ref,
                 gate_ref, ksum_ref, *, transposed_kv):
    hb = _rmsnorm(x_ref[...], g_ref[...]).astype(BF16)
    cos = cos_ref[...]
    sin = sin_ref[...]

    def proj(lo, n):
        return jnp.dot(hb, w_ref[:, lo:lo + n], preferred_element_type=F32)

    u_ref[...] = proj(OFF_U, S5_WIDTH)
    qm_ref[...] = _rope(proj(OFF_QM, MOBA_WIDTH), cos, sin)
    km = _rope(proj(OFF_KM, MOBA_WIDTH), cos, sin)
    kb_ref[...] = km.astype(BF16)
    ksum_ref[0] = jnp.sum(km, axis=0, keepdims=True)
    vm = proj(OFF_VM, MOBA_WIDTH)
    if transposed_kv:
        km, vm = km.T, vm.T
    km_ref[...] = km
    vm_ref[...] = vm
    vb_ref[...] = vm.astype(BF16)
    qr_ref[...] = _rope(proj(OFF_QR, RET_QK_WIDTH), cos, sin)
    kr_ref[...] = _rope(proj(OFF_KR, RET_QK_WIDTH), cos, sin) * (RET_DK ** -0.5)
    vr_ref[...] = proj(OFF_VR, RET_V_WIDTH)
    gr = proj(OFF_GR, RET_V_WIDTH)
    gr_ref[...] = gr * jax.nn.sigmoid(gr)
    for i in range(3):
        gate_ref[:, i * D_MODEL:(i + 1) * D_MODEL] = jax.nn.sigmoid(proj(OFF_GATES + i * D_MODEL, D_MODEL))


def _inproj(x, g, w, cos, sin, seq_tiles=None):
    n = x.shape[0]
    nt = n // ROW_TILE
    row = lambda width: pl.BlockSpec((ROW_TILE, width), lambda i: (i, 0))
    widths = (S5_WIDTH, MOBA_WIDTH, MOBA_WIDTH, MOBA_WIDTH, MOBA_WIDTH, MOBA_WIDTH,
              RET_QK_WIDTH, RET_QK_WIDTH, RET_V_WIDTH, RET_V_WIDTH, 3 * D_MODEL)
    dtypes = (F32, F32, F32, F32, BF16, BF16, F32, F32, F32, F32, F32)
    out_shape = [jax.ShapeDtypeStruct((n, wd), dt) for wd, dt in zip(widths, dtypes)]
    out_shape.append(jax.ShapeDtypeStruct((nt, 1, MOBA_WIDTH), F32))
    out_specs = [row(wd) for wd in widths]
    out_specs.append(pl.BlockSpec((1, 1, MOBA_WIDTH), lambda i: (i, 0, 0)))
    if seq_tiles is not None:
        n_seq = nt // seq_tiles
        kv_t = pl.BlockSpec((None, MOBA_WIDTH, ROW_TILE), lambda i: (i // seq_tiles, 0, i % seq_tiles))
        kv_t_shape = (n_seq, MOBA_WIDTH, seq_tiles * ROW_TILE)
        out_shape[2] = out_shape[3] = jax.ShapeDtypeStruct(kv_t_shape, F32)
        out_specs[2] = out_specs[3] = kv_t
        out_shape[5] = jax.ShapeDtypeStruct((n_seq, seq_tiles, MOBA_WIDTH, ROW_TILE), BF16)
        out_specs[5] = pl.BlockSpec((None, None, MOBA_WIDTH, ROW_TILE),
                                    lambda i: (i // seq_tiles, i % seq_tiles, 0, 0))
    return pl.pallas_call(
        functools.partial(_inproj_body, transposed_kv=seq_tiles is not None),
        out_shape=out_shape,
        grid=(nt,),
        in_specs=[row(D_MODEL), _const_spec((1, D_MODEL)), _const_spec((D_MODEL, IN_COLS)),
                  row(LANES), row(LANES)],
        out_specs=out_specs,
        compiler_params=_params(1),
        name="inproj",
    )(x, g, w, cos, sin)


def _gelu_tanh(x):
    cdf = 0.5 * (1.0 + jnp.tanh(math.sqrt(2.0 / math.pi) * (x + 0.044715 * (x * x * x))))
    return x * cdf


def _s5_body(u_ref, x0r_ref, x0i_ref, lamr_ref, lami_ref, ljr_ref, lji_ref, wbr_ref, wbi_ref,
             wcr_ref, wci_ref, d_ref, wg_ref,
             y_ref, xlr_ref, xli_ref,
             xr_s, xi_s, cr_s, ci_s, carr_s, cari_s, *, steps, chained):
    n_bundles = S5_GROUPS // S5_BUNDLE
    in_w = S5_BUNDLE * S5_GROUP
    st_w = S5_BUNDLE * S5_STATE
    u = u_ref[...]
    ub = u.astype(BF16)
    for j in range(n_bundles):
        uj = ub[:, j * in_w:(j + 1) * in_w]
        xr_s[:, j * st_w:(j + 1) * st_w] = jnp.dot(uj, wbr_ref[j], preferred_element_type=F32)
        xi_s[:, j * st_w:(j + 1) * st_w] = jnp.dot(uj, wbi_ref[j], preferred_element_type=F32)

    def scan(init_r, init_i, lo, store):
        lanes = slice(lo, lo + S5_LANE_GROUP)
        lr = lamr_ref[:, lanes]
        li = lami_ref[:, lanes]

        def step(t, carry):
            xr, xi = carry
            rows = pl.ds(pl.multiple_of(t * SUBLANES, SUBLANES), SUBLANES)
            nr = lr * xr - li * xi + xr_s[rows, lanes]
            ni = lr * xi + li * xr + xi_s[rows, lanes]
            if store:
                xr_s[rows, lanes] = nr
                xi_s[rows, lanes] = ni
            return nr, ni

        return lax.fori_loop(0, steps, step, (init_r, init_i), unroll=4)

    if chained:
        @pl.when(pl.program_id(1) == 0)
        def _():
            carr_s[...] = x0r_ref[...]
            cari_s[...] = x0i_ref[...]

    zeros = jnp.zeros((SUBLANES, S5_LANE_GROUP), F32)
    for lo in range(0, S5_FLAT, S5_LANE_GROUP):
        lanes = slice(lo, lo + S5_LANE_GROUP)
        if chained:
            end_r, end_i = scan(zeros, zeros, lo, store=False)
            c_r = carr_s[:, lanes]
            c_i = cari_s[:, lanes]
            ljr = ljr_ref[:, lanes]
            lji = lji_ref[:, lanes]
            for s in range(SUBLANES):
                cr_s[s:s + 1, lanes] = c_r
                ci_s[s:s + 1, lanes] = c_i
                n_r = ljr * c_r - lji * c_i + end_r[s:s + 1, :]
                n_i = ljr * c_i + lji * c_r + end_i[s:s + 1, :]
                c_r, c_i = n_r, n_i
            carr_s[:, lanes] = c_r
            cari_s[:, lanes] = c_i
            scan(cr_s[:, lanes], ci_s[:, lanes], lo, store=True)
        else:
            f_r, f_i = scan(x0r_ref[:, lanes], x0i_ref[:, lanes], lo, store=True)
            xlr_ref[:, lanes] = f_r
            xli_ref[:, lanes] = f_i
    if chained:
        xlr_ref[...] = carr_s[...]
        xli_ref[...] = cari_s[...]

    ys = []
    for j in range(n_bundles):
        sl = slice(j * st_w, (j + 1) * st_w)
        ys.append(_mm(xr_s[:, sl], wcr_ref[j]) + _mm(xi_s[:, sl], wci_ref[j]))
    y = jnp.concatenate(ys, axis=1) + d_ref[...] * u
    y = _gelu_tanh(y)
    y_ref[...] = y * jax.nn.sigmoid(_mm(y, wg_ref[...]))


def _s5_mixer(u, x0r, x0i, prm, *, steps, chained):
    n_groups, rows, _ = u.shape
    t = SUBLANES * steps
    n_chunks = rows // t

    def swap_row_order(a, inner, outer):
        return a.reshape(n_groups, n_chunks, inner, outer, -1).transpose(0, 1, 3, 2, 4).reshape(a.shape)

    u = swap_row_order(u, SUBLANES, steps)
    r0 = x0r.shape[1]
    chunk = pl.BlockSpec((None, t, S5_WIDTH), lambda b, c: (b, c, 0))
    state = pl.BlockSpec((None, r0, S5_FLAT), lambda b, c: (b, 0, 0))
    n_bundles = S5_GROUPS // S5_BUNDLE
    in_w = S5_BUNDLE * S5_GROUP
    st_w = S5_BUNDLE * S5_STATE
    body = functools.partial(_s5_body, steps=steps, chained=chained)
    y, xl_r, xl_i = pl.pallas_call(
        body,
        out_shape=[jax.ShapeDtypeStruct(u.shape, F32),
                   jax.ShapeDtypeStruct(x0r.shape, F32), jax.ShapeDtypeStruct(x0r.shape, F32)],
        grid=(n_groups, n_chunks),
        in_specs=[chunk, state, state,
                  _const_spec((SUBLANES, S5_FLAT)), _const_spec((SUBLANES, S5_FLAT)),
                  _const_spec((1, S5_FLAT)), _const_spec((1, S5_FLAT)),
                  _const_spec((n_bundles, in_w, st_w)), _const_spec((n_bundles, in_w, st_w)),
                  _const_spec((n_bundles, st_w, in_w)), _const_spec((n_bundles, st_w, in_w)),
                  _const_spec((1, S5_WIDTH)), _const_spec((S5_WIDTH, S5_WIDTH))],
        out_specs=[chunk, state, state],
        scratch_shapes=[pltpu.VMEM((t, S5_FLAT), F32), pltpu.VMEM((t, S5_FLAT), F32),
                        pltpu.VMEM((SUBLANES, S5_FLAT), F32), pltpu.VMEM((SUBLANES, S5_FLAT), F32),
                        pltpu.VMEM((1, S5_FLAT), F32), pltpu.VMEM((1, S5_FLAT), F32)],
        compiler_params=_params(2),
        name="s5_chained" if chained else "s5_independent",
    )(u, x0r, x0i, prm["lam_r8"], prm["lam_i8"], prm["lj_r"][steps], prm["lj_i"][steps],
      prm["wb_r"], prm["wb_i"], prm["wc_r"], prm["wc_i"], prm["d"], prm["w_glu"])
    return swap_row_order(y, steps, SUBLANES), xl_r, xl_i


def _s5_prepare(lam_re, lam_im, log_dt, b_re, b_im, c_re, c_im, d_skip, w_glu, step_counts):
    a = jnp.minimum(lam_re.astype(F32), -1e-4)
    b = lam_im.astype(F32)
    dt = jnp.exp(log_dt.astype(F32))[:, None]

    def lam_bar_power(n):
        mag = jnp.exp(a * dt * n)
        return mag * jnp.cos(b * dt * n), mag * jnp.sin(b * dt * n)

    e_r, e_i = lam_bar_power(1.0)
    den = a * a + b * b
    cf_r = ((e_r - 1.0) * a + e_i * b) / den
    cf_i = (e_i * a - (e_r - 1.0) * b) / den
    bb_r = cf_r[..., None] * b_re.astype(F32) - cf_i[..., None] * b_im.astype(F32)
    bb_i = cf_r[..., None] * b_im.astype(F32) + cf_i[..., None] * b_re.astype(F32)
    n_bundles = S5_GROUPS // S5_BUNDLE
    eye = jnp.eye(S5_BUNDLE, dtype=F32)

    def expand_b(t):
        t = t.reshape(n_bundles, S5_BUNDLE, S5_STATE, S5_GROUP)
        w = jnp.einsum("jgpc,gh->jgchp", t, eye)
        return w.reshape(n_bundles, S5_BUNDLE * S5_GROUP, S5_BUNDLE * S5_STATE).astype(BF16)

    def expand_c(t):
        t = t.reshape(n_bundles, S5_BUNDLE, S5_GROUP, S5_STATE)
        w = jnp.einsum("jgcp,gh->jgphc", t, eye)
        return w.reshape(n_bundles, S5_BUNDLE * S5_STATE, S5_BUNDLE * S5_GROUP).astype(BF16)

    flat = lambda t: t.reshape(1, S5_FLAT)
    lj_r, lj_i = {}, {}
    for steps in step_counts:
        p_r, p_i = lam_bar_power(float(steps))
        lj_r[steps] = flat(p_r)
        lj_i[steps] = flat(p_i)
    return {
        "lam_r8": jnp.broadcast_to(flat(e_r), (SUBLANES, S5_FLAT)),
        "lam_i8": jnp.broadcast_to(flat(e_i), (SUBLANES, S5_FLAT)),
        "lj_r": lj_r, "lj_i": lj_i,
        "wb_r": expand_b(bb_r), "wb_i": expand_b(bb_i),
        "wc_r": expand_c(c_re.astype(F32)), "wc_i": expand_c(-c_im.astype(F32)),
        "d": d_skip.astype(F32).reshape(1, S5_WIDTH),
        "w_glu": w_glu.astype(BF16),
    }


def _retention_body(q_ref, k_ref, v_ref, g_ref, s0_ref, dm_ref, qd_ref, kd_ref, cd_ref,
                    y_ref, sout_ref, s_s):
    @pl.when(pl.program_id(1) == 0)
    def _():
        s_s[...] = s0_ref[...]

    for h in range(RET_HEADS):
        qk = slice(h * RET_DK, (h + 1) * RET_DK)
        vv = slice(h * RET_DV, (h + 1) * RET_DV)
        q = q_ref[:, qk].astype(BF16)
        k = k_ref[:, qk]
        v = v_ref[:, vv].astype(BF16)
        s = s_s[h]
        inner = lax.dot_general(q, k.astype(BF16), NT_DIMS, preferred_element_type=F32) * dm_ref[h]
        o = (jnp.dot(inner.astype(BF16), v, preferred_element_type=F32)
             + jnp.dot(q, s.astype(BF16), preferred_element_type=F32) * qd_ref[h])
        k_dec = (k * kd_ref[h]).astype(BF16)
        s_s[h] = s * cd_ref[h] + lax.dot_general(k_dec, v, TN_DIMS, preferred_element_type=F32)
        mu = jnp.mean(o, axis=-1, keepdims=True)
        var = jnp.mean(jnp.square(o - mu), axis=-1, keepdims=True)
        y_ref[:, vv] = g_ref[:, vv] * ((o - mu) * lax.rsqrt(var + NORM_EPS))
    sout_ref[...] = s_s[...]


def _retention(q, k, v, g, s0, chunk):
    bsz, seq_len, _ = q.shape
    n_chunks = seq_len // chunk
    log_g = jnp.log(1.0 - 2.0 ** (-5.0 - jnp.arange(RET_HEADS, dtype=F32)))
    idx = jnp.arange(chunk, dtype=F32)
    diff = idx[:, None] - idx[None, :]
    decay_mask = jnp.where(diff >= 0, jnp.exp(log_g[:, None, None] * jnp.maximum(diff, 0.0)), 0.0)
    q_decay = jnp.exp(log_g[:, None] * (idx + 1.0))[:, :, None]
    k_decay = jnp.exp(log_g[:, None] * (chunk - 1.0 - idx))[:, :, None]
    chunk_decay = jnp.broadcast_to(jnp.exp(log_g * chunk)[:, None, None], (RET_HEADS, 1, RET_DV))
    rows = lambda width: pl.BlockSpec((None, chunk, width), lambda b, c: (b, c, 0))
    state = pl.BlockSpec((None, RET_HEADS, RET_DK, RET_DV), lambda b, c: (b, 0, 0, 0))
    return pl.pallas_call(
        _retention_body,
        out_shape=[jax.ShapeDtypeStruct(v.shape, F32), jax.ShapeDtypeStruct(s0.shape, F32)],
        grid=(bsz, n_chunks),
        in_specs=[rows(RET_QK_WIDTH), rows(RET_QK_WIDTH), rows(RET_V_WIDTH), rows(RET_V_WIDTH), state,
                  _const_spec((RET_HEADS, chunk, chunk)), _const_spec((RET_HEADS, chunk, 1)),
                  _const_spec((RET_HEADS, chunk, 1)), _const_spec((RET_HEADS, 1, RET_DV))],
        out_specs=[rows(RET_V_WIDTH), state],
        scratch_shapes=[pltpu.VMEM((RET_HEADS, RET_DK, RET_DV), F32)],
        compiler_params=_params(2),
        name=f"retention_c{chunk}",
    )(q, k, v, g, s0.astype(F32), decay_mask, q_decay, k_decay, chunk_decay)


def _topk_rank(gate, axis, n):
    idx = lax.broadcasted_iota(jnp.int32, gate.shape, axis)
    rank = jnp.zeros(gate.shape, jnp.int32)
    for other in range(n):
        g_o = gate[:, other:other + 1] if axis == 1 else gate[other:other + 1, :]
        beats = (g_o > gate) | ((g_o == gate) & (other < idx))
        rank = rank + beats.astype(jnp.int32)
    return rank


def _moba_prompt_body(q_ref, k_ref, vt_ref, ksum_ref, o_ref, w_s, sel_s, s_s, p_s, acc_s, *, n_blocks):
    qb = pl.program_id(1)
    own = (qb * MOBA_QBLOCK) // MOBA_BLOCK
    q_off = (qb * MOBA_QBLOCK) % MOBA_BLOCK
    key_i = lax.broadcasted_iota(jnp.int32, (MOBA_BLOCK, MOBA_QBLOCK), 0)
    q_i = lax.broadcasted_iota(jnp.int32, (MOBA_BLOCK, MOBA_QBLOCK), 1)
    causal = key_i <= q_off + q_i
    blk = lax.broadcasted_iota(jnp.int32, (n_blocks, MOBA_QBLOCK), 0)
    scale = MOBA_HEAD_DIM ** -0.5
    q_t = q_ref[...].T
    zero_half = jnp.zeros((MOBA_HEAD_DIM, MOBA_QBLOCK), F32)
    for h in range(MOBA_HEADS):
        hd = slice(h * MOBA_HEAD_DIM, (h + 1) * MOBA_HEAD_DIM)
        q_h = q_t[hd, :]
        kmean = ksum_ref[:, hd] * (1.0 / MOBA_BLOCK)
        gate = jnp.dot(kmean, q_h, precision=lax.Precision.HIGHEST, preferred_element_type=F32)
        gate = jnp.where(blk < own, gate, NEG_INF)
        sel_s[h] = ((_topk_rank(gate, 0, n_blocks) < MOBA_TOPK) & (blk < own)).astype(F32)
        halves = [q_h * scale, zero_half] if h % 2 == 0 else [zero_half, q_h * scale]
        w_s[h] = jnp.concatenate(halves, axis=0).astype(BF16)
    heads = range(MOBA_HEADS)
    hd_of = lambda h: slice(h * MOBA_HEAD_DIM, (h + 1) * MOBA_HEAD_DIM)

    def scores(slot, n):
        rows = pl.ds(pl.multiple_of(n * MOBA_BLOCK, MOBA_BLOCK), MOBA_BLOCK)
        for h in heads:
            pair = slice((h // 2) * LANES, (h // 2 + 1) * LANES)
            s_s[slot, h] = jnp.dot(k_ref[rows, pair], w_s[h], preferred_element_type=F32)

    scores(0, own)
    m, l = [], []
    for h in heads:
        s = jnp.where(causal, s_s[0, h], NEG_INF)
        m_h = jnp.max(s, axis=0, keepdims=True)
        p = jnp.exp(s - m_h)
        p_s[h, 0:MOBA_BLOCK, :] = p.astype(BF16)
        m.append(m_h)
        l.append(jnp.sum(p, axis=0, keepdims=True))
    for h in heads:
        acc_s[hd_of(h), :] = jnp.dot(vt_ref[own, hd_of(h), :], p_s[h, 0:MOBA_BLOCK, :],
                                     preferred_element_type=F32)

    def past_pair(t, carry):
        m, l = carry
        blocks = (2 * t, jnp.minimum(2 * t + 1, n_blocks - 1))
        valid = (True, 2 * t + 1 < own)
        for slot, n in enumerate(blocks):
            scores(slot, n)
        m_out, l_out, alphas = [], [], []
        for h in heads:
            chosen = [(sel_s[h, pl.ds(n, 1), :] > 0.5) & ok for n, ok in zip(blocks, valid)]
            tops = [jnp.where(c, jnp.max(s_s[slot, h], axis=0, keepdims=True), NEG_INF)
                    for slot, c in enumerate(chosen)]
            m_new = jnp.maximum(m[h], jnp.maximum(tops[0], tops[1]))
            alpha = jnp.exp(m[h] - m_new)
            l_new = alpha * l[h]
            for slot, c in enumerate(chosen):
                p = jnp.exp(s_s[slot, h] - jnp.where(c, m_new, -NEG_INF))
                p_s[h, slot * MOBA_BLOCK:(slot + 1) * MOBA_BLOCK, :] = p.astype(BF16)
                l_new = l_new + jnp.sum(p, axis=0, keepdims=True)
            m_out.append(m_new)
            l_out.append(l_new)
            alphas.append(alpha)
        for h in heads:
            vt = jnp.concatenate([vt_ref[n, hd_of(h), :] for n in blocks], axis=1)
            acc_s[hd_of(h), :] = alphas[h] * acc_s[hd_of(h), :] + jnp.dot(vt, p_s[h],
                                                                        preferred_element_type=F32)
        return tuple(m_out), tuple(l_out)

    m, l = lax.fori_loop(0, (own + 1) // 2, past_pair, (tuple(m), tuple(l)))
    o_ref[...] = jnp.concatenate([acc_s[hd_of(h), :] / l[h] for h in heads], axis=0).T


def _moba_prompt(q, kb, vt, ksum):
    bsz, seq_len, _ = q.shape
    n_blocks = seq_len // MOBA_BLOCK
    qblock = pl.BlockSpec((None, MOBA_QBLOCK, MOBA_WIDTH), lambda b, i: (b, i, 0))
    return pl.pallas_call(
        functools.partial(_moba_prompt_body, n_blocks=n_blocks),
        out_shape=jax.ShapeDtypeStruct(q.shape, F32),
        grid=(bsz, seq_len // MOBA_QBLOCK),
        in_specs=[qblock,
                  pl.BlockSpec((None, seq_len, MOBA_WIDTH), lambda b, i: (b, 0, 0)),
                  pl.BlockSpec((None, n_blocks, MOBA_WIDTH, MOBA_BLOCK), lambda b, i: (b, 0, 0, 0)),
                  pl.BlockSpec((None, n_blocks, MOBA_WIDTH), lambda b, i: (b, 0, 0))],
        out_specs=qblock,
        scratch_shapes=[pltpu.VMEM((MOBA_HEADS, LANES, MOBA_QBLOCK), BF16),
                        pltpu.VMEM((MOBA_HEADS, n_blocks, MOBA_QBLOCK), F32),
                        pltpu.VMEM((2, MOBA_HEADS, MOBA_BLOCK, MOBA_QBLOCK), F32),
                        pltpu.VMEM((MOBA_HEADS, 2 * MOBA_BLOCK, MOBA_QBLOCK), BF16),
                        pltpu.VMEM((MOBA_WIDTH, MOBA_QBLOCK), F32)],
        compiler_params=_params(2),
        name="moba_prompt",
    )(q, kb, vt, ksum)


def _moba_sample_body(pt_ref, q_ref, kn_ref, vn_ref, *rest, n_pages, dec_seq):
    del pt_ref
    pages_per_block = MOBA_BLOCK // PAGE_SIZE
    pps = SAMPLE_BLOCKS_PER_STEP * pages_per_block
    ck_refs = rest[:pps]
    cv_refs = rest[pps:2 * pps]
    o_ref, qbd_s, qs_s, kmt_s, m_s, l_s, acc_s = rest[2 * pps:]
    step = pl.program_id(1)
    n_steps = n_pages // pps
    n_blocks = n_pages // pages_per_block
    n_cols = MOBA_HEADS * dec_seq
    scale = MOBA_HEAD_DIM ** -0.5
    r_i = lax.broadcasted_iota(jnp.int32, (n_cols, MOBA_WIDTH), 0)
    c_i = lax.broadcasted_iota(jnp.int32, (n_cols, MOBA_WIDTH), 1)
    head_diag = (r_i // dec_seq) == (c_i // MOBA_HEAD_DIM)

    @pl.when(step == 0)
    def _():
        q_rep = jnp.concatenate([q_ref[...]] * MOBA_HEADS, axis=0)
        q_bd = jnp.where(head_diag, q_rep, 0.0)
        qbd_s[...] = q_bd
        qs_s[...] = (q_bd * scale).astype(BF16)
        m_s[...] = jnp.full(m_s.shape, NEG_INF, F32)
        l_s[...] = jnp.zeros(l_s.shape, F32)

    qs = qs_s[...]
    blk_lane = lax.broadcasted_iota(jnp.int32, (n_cols, n_blocks), 1)
    kmt_lane = lax.broadcasted_iota(jnp.int32, (MOBA_WIDTH, n_blocks), 1)
    kmeans, tops, sums = [], [], []
    for j in range(SAMPLE_BLOCKS_PER_STEP):
        kt = jnp.concatenate([ck_refs[j * pages_per_block + i][...].reshape(MOBA_WIDTH, PAGE_SIZE)
                              for i in range(pages_per_block)], axis=1)
        vt = jnp.concatenate([cv_refs[j * pages_per_block + i][...].reshape(MOBA_WIDTH, PAGE_SIZE)
                              for i in range(pages_per_block)], axis=1)
        kmeans.append(jnp.sum(kt, axis=1, keepdims=True) * (1.0 / MOBA_BLOCK))
        s = jnp.dot(qs, kt.astype(BF16), preferred_element_type=F32)
        m_n = jnp.max(s, axis=1, keepdims=True)
        p = jnp.exp(s - m_n)
        tops.append(m_n)
        sums.append(jnp.sum(p, axis=1, keepdims=True))
        acc_s[step * SAMPLE_BLOCKS_PER_STEP + j] = lax.dot_general(
            p.astype(BF16), vt.astype(BF16), NT_DIMS, preferred_element_type=F32)
    kmt, m_all, l_all = kmt_s[...], m_s[...], l_s[...]
    for j in range(SAMPLE_BLOCKS_PER_STEP):
        block = step * SAMPLE_BLOCKS_PER_STEP + j
        kmt = jnp.where(kmt_lane == block, kmeans[j], kmt)
        m_all = jnp.where(blk_lane == block, tops[j], m_all)
        l_all = jnp.where(blk_lane == block, sums[j], l_all)
    kmt_s[...] = kmt
    m_s[...] = m_all
    l_s[...] = l_all

    @pl.when(step == n_steps - 1)
    def _():
        gate = jnp.dot(qbd_s[...], kmt_s[...], precision=lax.Precision.HIGHEST,
                       preferred_element_type=F32)
        sel = _topk_rank(gate, 1, n_blocks) < MOBA_TOPK
        s_own = lax.dot_general(qs_s[...], kn_ref[...].astype(BF16), NT_DIMS, preferred_element_type=F32)
        o_row = lax.broadcasted_iota(jnp.int32, s_own.shape, 0)
        o_lane = lax.broadcasted_iota(jnp.int32, s_own.shape, 1)
        s_own = jnp.where(o_lane <= (o_row % dec_seq), s_own, NEG_INF)
        m_own = jnp.max(s_own, axis=1, keepdims=True)
        p_own = jnp.exp(s_own - m_own)
        l_own = jnp.sum(p_own, axis=1, keepdims=True)
        acc_own = jnp.dot(p_own.astype(BF16), vn_ref[...].astype(BF16), preferred_element_type=F32)
        m_all = m_s[...]
        top = jnp.maximum(m_own, jnp.max(jnp.where(sel, m_all, NEG_INF), axis=1, keepdims=True))
        w = jnp.exp(jnp.where(sel, m_all - top, NEG_INF))
        w_own = jnp.exp(m_own - top)
        denom = w_own * l_own + jnp.sum(w * l_s[...], axis=1, keepdims=True)
        out = w_own * acc_own
        for n in range(n_blocks):
            out = out + w[:, n:n + 1] * acc_s[n]
        out = jnp.where(head_diag, out / denom, 0.0)
        res = out[0:dec_seq]
        for h in range(1, MOBA_HEADS):
            res = res + out[h * dec_seq:(h + 1) * dec_seq]
        o_ref[...] = res


def _moba_sample(layer, page_table, q, k_new, v_new, cache_kt, cache_vt):
    bsz, dec_seq, _ = q.shape
    n_pages = page_table.shape[1]
    pages_per_block = MOBA_BLOCK // PAGE_SIZE
    pps = SAMPLE_BLOCKS_PER_STEP * pages_per_block
    n_steps = n_pages // pps
    n_blocks = n_pages // pages_per_block
    n_cols = MOBA_HEADS * dec_seq
    tok = pl.BlockSpec((None, dec_seq, MOBA_WIDTH), lambda b, s, pt: (b, 0, 0))

    def page_spec(i):
        return pl.BlockSpec((None, None, MOBA_HEADS, MOBA_HEAD_DIM, PAGE_SIZE),
                            lambda b, s, pt: (layer, pt[b, s * pps + i], 0, 0, 0))

    grid_spec = pltpu.PrefetchScalarGridSpec(
        num_scalar_prefetch=1,
        grid=(bsz, n_steps),
        in_specs=[tok, tok, tok] + [page_spec(i) for i in range(pps)] * 2,
        out_specs=tok,
        scratch_shapes=[pltpu.VMEM((n_cols, MOBA_WIDTH), F32),
                        pltpu.VMEM((n_cols, MOBA_WIDTH), BF16),
                        pltpu.VMEM((MOBA_WIDTH, n_blocks), F32),
                        pltpu.VMEM((n_cols, n_blocks), F32),
                        pltpu.VMEM((n_cols, n_blocks), F32),
                        pltpu.VMEM((n_blocks, n_cols, MOBA_WIDTH), F32)],
    )
    return pl.pallas_call(
        functools.partial(_moba_sample_body, n_pages=n_pages, dec_seq=dec_seq),
        out_shape=jax.ShapeDtypeStruct(q.shape, F32),
        grid_spec=grid_spec,
        compiler_params=_params(2),
        name="moba_sample",
    )(page_table, q, k_new, v_new, *([cache_kt] * pps), *([cache_vt] * pps))


def _merge_body(x_ref, ys_ref, ym_ref, yr_ref, gate_ref, wbs_ref, wbm_ref, wbr_ref, wo_ref, g2_ref,
                wu_ref, wd_ref, gf_ref, o_ref, *, final):
    merged = (gate_ref[:, 0:D_MODEL] * _mm(ys_ref[...], wbs_ref[...])
              + gate_ref[:, D_MODEL:2 * D_MODEL] * _mm(ym_ref[...], wbm_ref[...])
              + gate_ref[:, 2 * D_MODEL:3 * D_MODEL] * _mm(yr_ref[...], wbr_ref[...]))
    x = x_ref[...] + _mm(merged, wo_ref[...])
    up = _mm(_rmsnorm(x, g2_ref[...]), wu_ref[...])
    x = x + _mm(jnp.square(jnp.maximum(up, 0.0)), wd_ref[...])
    if final:
        x = _rmsnorm(x, gf_ref[...])
    o_ref[...] = x


def _merge(x, y_s5, y_moba, y_ret, gates, w, final):
    n = x.shape[0]
    row = lambda width: pl.BlockSpec((ROW_TILE, width), lambda i: (i, 0))
    return pl.pallas_call(
        functools.partial(_merge_body, final=final),
        out_shape=jax.ShapeDtypeStruct(x.shape, F32),
        grid=(n // ROW_TILE,),
        in_specs=[row(D_MODEL), row(S5_WIDTH), row(MOBA_WIDTH), row(RET_V_WIDTH), row(3 * D_MODEL),
                  _const_spec((S5_WIDTH, D_MODEL)), _const_spec((MOBA_WIDTH, D_MODEL)),
                  _const_spec((RET_V_WIDTH, D_MODEL)), _const_spec((D_MODEL, D_MODEL)),
                  _const_spec((1, D_MODEL)), _const_spec((D_MODEL, D_FF)), _const_spec((D_FF, D_MODEL)),
                  _const_spec((1, D_MODEL))],
        out_specs=row(D_MODEL),
        compiler_params=_params(1),
        name="merge_mlp",
    )(x, y_s5, y_moba, y_ret, gates, w["w_br_s5"], w["w_br_moba"], w["w_br_ret"], w["w_out"],
      w["norm2_g"], w["w_mlp_up"], w["w_mlp_down"], w["final_g"])


def _rope_tables(pos):
    half = MOBA_HEAD_DIM // 2
    freqs = ROPE_THETA ** (-jnp.arange(half, dtype=F32) / half)
    ang = pos.astype(F32)[:, None] * freqs[None, :]
    cos = jnp.cos(ang)
    sin = jnp.sin(ang)
    reps = LANES // MOBA_HEAD_DIM
    return (jnp.concatenate([cos, cos] * reps, axis=1), jnp.concatenate([-sin, sin] * reps, axis=1))


def kernel(x_prompt, x_sample, cache_k, cache_v, state_s5_re, state_s5_im, state_ret, page_table,
           norm1_g, w_in, s5_lambda_re, s5_lambda_im, s5_log_dt, s5_b_re, s5_b_im, s5_c_re, s5_c_im,
           s5_d, s5_w_glu, w_br_s5, w_br_moba, w_br_ret, w_out, norm2_g, w_mlp_up, w_mlp_down, final_g):
    bsz_p, seq_p, _ = x_prompt.shape
    bsz_s, seq_s, _ = x_sample.shape
    depth = w_in.shape[0]
    n_pages = page_table.shape[1]
    past_len = n_pages * PAGE_SIZE
    n_p = bsz_p * seq_p
    n_s = bsz_s * seq_s
    prompt_steps = ROW_TILE // SUBLANES
    assert seq_p % ROW_TILE == 0 and n_s % ROW_TILE == 0 and seq_p % RET_CHUNK == 0
    assert ROW_TILE == MOBA_BLOCK
    assert n_pages % (SAMPLE_BLOCKS_PER_STEP * MOBA_BLOCK // PAGE_SIZE) == 0
    assert seq_s == SUBLANES and bsz_s % SUBLANES == 0 and seq_s % RET_CHUNK != 0

    cos_p, sin_p = _rope_tables(jnp.tile(jnp.arange(seq_p, dtype=jnp.int32), bsz_p))
    cos_s, sin_s = _rope_tables(jnp.tile(past_len + jnp.arange(seq_s, dtype=jnp.int32), bsz_s))
    zeros_s5 = jnp.zeros((bsz_p, 1, S5_FLAT), F32)
    zeros_ret = jnp.zeros((bsz_p, RET_HEADS, RET_DK, RET_DV), F32)
    cache_kt = cache_k.transpose(0, 1, 3, 4, 2)
    cache_vt = cache_v.transpose(0, 1, 3, 4, 2)

    hp = x_prompt.reshape(n_p, D_MODEL)
    hs = x_sample.reshape(n_s, D_MODEL)
    outs = {name: [] for name in ("kp", "vp", "ks", "vs", "s5rp", "s5ip", "s5rs", "s5is", "retp", "rets")}
    for l in range(depth):
        last = l == depth - 1
        w = {"w_br_s5": w_br_s5[l].astype(BF16), "w_br_moba": w_br_moba[l].astype(BF16),
             "w_br_ret": w_br_ret[l].astype(BF16), "w_out": w_out[l].astype(BF16),
             "norm2_g": norm2_g[l].reshape(1, D_MODEL), "w_mlp_up": w_mlp_up[l].astype(BF16),
             "w_mlp_down": w_mlp_down[l].astype(BF16), "final_g": final_g.reshape(1, D_MODEL)}
        g1 = norm1_g[l].reshape(1, D_MODEL)
        w_in_l = w_in[l].astype(BF16)
        s5p = _s5_prepare(s5_lambda_re[l], s5_lambda_im[l], s5_log_dt[l], s5_b_re[l], s5_b_im[l],
                          s5_c_re[l], s5_c_im[l], s5_d[l], s5_w_glu[l], (prompt_steps, seq_s))

        n_blk = seq_p // MOBA_BLOCK
        (u, qm, km_t, vm_t, kb, vt, qr, kr, vr, gr, gates, ksum) = _inproj(hp, g1, w_in_l, cos_p, sin_p,
                                                                           seq_tiles=n_blk)
        seq = lambda t: t.reshape(bsz_p, seq_p, t.shape[-1])
        y_s5, s5r, s5i = _s5_mixer(seq(u), zeros_s5, zeros_s5, s5p, steps=prompt_steps, chained=True)
        y_moba = _moba_prompt(seq(qm), seq(kb), vt, ksum.reshape(bsz_p, n_blk, MOBA_WIDTH))
        y_ret, ret_s = _retention(seq(qr), seq(kr), seq(vr), seq(gr), zeros_ret, RET_CHUNK)
        hp = _merge(hp, y_s5.reshape(n_p, -1), y_moba.reshape(n_p, -1), y_ret.reshape(n_p, -1), gates, w, last)
        per_head = lambda t: t.reshape(bsz_p, MOBA_HEADS, MOBA_HEAD_DIM, seq_p).transpose(0, 3, 1, 2)
        outs["kp"].append(per_head(km_t))
        outs["vp"].append(per_head(vm_t))
        outs["s5rp"].append(s5r.reshape(bsz_p, S5_GROUPS, S5_STATE))
        outs["s5ip"].append(s5i.reshape(bsz_p, S5_GROUPS, S5_STATE))
        outs["retp"].append(ret_s)

        (u, qm, km, vm, kb, vb, qr, kr, vr, gr, gates, ksum) = _inproj(hs, g1, w_in_l, cos_s, sin_s)
        grp = lambda t: t.reshape(bsz_s // SUBLANES, SUBLANES * seq_s, t.shape[-1])
        st = lambda t: t.astype(F32).reshape(bsz_s // SUBLANES, SUBLANES, S5_FLAT)
        y_s5, s5r, s5i = _s5_mixer(grp(u), st(state_s5_re[l]), st(state_s5_im[l]), s5p,
                                   steps=seq_s, chained=False)
        seq = lambda t: t.reshape(bsz_s, seq_s, t.shape[-1])
        y_moba = _moba_sample(l, page_table, seq(qm), seq(km), seq(vm), cache_kt, cache_vt)
        y_ret, ret_s = _retention(seq(qr), seq(kr), seq(vr), seq(gr), state_ret[l], seq_s)
        hs = _merge(hs, y_s5.reshape(n_s, -1), y_moba.reshape(n_s, -1), y_ret.reshape(n_s, -1), gates, w, last)
        outs["ks"].append(km.reshape(bsz_s, seq_s, MOBA_HEADS, MOBA_HEAD_DIM))
        outs["vs"].append(vm.reshape(bsz_s, seq_s, MOBA_HEADS, MOBA_HEAD_DIM))
        outs["s5rs"].append(s5r.reshape(bsz_s, S5_GROUPS, S5_STATE))
        outs["s5is"].append(s5i.reshape(bsz_s, S5_GROUPS, S5_STATE))
        outs["rets"].append(ret_s)

    stack = lambda name: jnp.stack(outs[name])
    return (hp.reshape(x_prompt.shape), hs.reshape(x_sample.shape),
            stack("kp"), stack("vp"), stack("ks"), stack("vs"),
            stack("s5rp"), stack("s5ip"), stack("s5rs"), stack("s5is"), stack("retp"), stack("rets"))
```

```python
import functools
import math

import jax
import jax.numpy as jnp
from jax import lax
from jax.experimental import pallas as pl
from jax.experimental.pallas import tpu as pltpu

F32 = jnp.float32
BF16 = jnp.bfloat16

D_MODEL = 1024
PAGE_SIZE = 128
S5_WIDTH = 512
S5_GROUP = 16
S5_GROUPS = 32
S5_STATE = 64
S5_FLAT = S5_GROUPS * S5_STATE
MOBA_HEADS = 8
MOBA_HEAD_DIM = 64
MOBA_WIDTH = 512
MOBA_BLOCK = 256
MOBA_TOPK = 3
MOBA_QBLOCK = 128
RET_HEADS = 4
RET_DK = 64
RET_DV = 128
RET_QK_WIDTH = 256
RET_V_WIDTH = 512
RET_CHUNK = 128
D_FF = 4096
ROPE_THETA = 10000.0
NORM_EPS = 1e-6
NEG_INF = -1e30
IN_SPLITS = (S5_WIDTH, MOBA_WIDTH, MOBA_WIDTH, MOBA_WIDTH, RET_QK_WIDTH, RET_QK_WIDTH,
             RET_V_WIDTH, RET_V_WIDTH, D_MODEL, D_MODEL, D_MODEL)
IN_COLS = sum(IN_SPLITS)
(OFF_U, OFF_QM, OFF_KM, OFF_VM, OFF_QR, OFF_KR, OFF_VR, OFF_GR, OFF_GATES) = (
    0, 512, 1024, 1536, 2048, 2304, 2560, 3072, 3584)

LANES = 128
SUBLANES = 8
ROW_TILE = 256
S5_BUNDLE = 8
S5_LANE_GROUP = 512
SAMPLE_BLOCKS_PER_STEP = 4
VMEM_LIMIT = 56 * 1024 * 1024

NT_DIMS = (((1,), (1,)), ((), ()))
TN_DIMS = (((0,), (0,)), ((), ()))


def _const_spec(shape):
    nd = len(shape)
    return pl.BlockSpec(shape, lambda *_: (0,) * nd, pipeline_mode=pl.Buffered(1))


def _params(n_axes):
    return pltpu.CompilerParams(dimension_semantics=("arbitrary",) * n_axes,
                                vmem_limit_bytes=VMEM_LIMIT)


def _rmsnorm(x, g):
    return x * lax.rsqrt(jnp.mean(x * x, axis=-1, keepdims=True) + NORM_EPS) * g


def _mm(a, w):
    return jnp.dot(a.astype(BF16), w, preferred_element_type=F32)


def _rope(t, cos, sin_signed):
    n = t.shape[1]
    reps = n // LANES
    c = jnp.concatenate([cos] * reps, axis=1)
    s = jnp.concatenate([sin_signed] * reps, axis=1)
    lane = lax.broadcasted_iota(jnp.int32, t.shape, 1)
    first_half = (lane % MOBA_HEAD_DIM) < (MOBA_HEAD_DIM // 2)
    partner = jnp.where(first_half, pltpu.roll(t, n - MOBA_HEAD_DIM // 2, 1),
                        pltpu.roll(t, MOBA_HEAD_DIM // 2, 1))
    return t * c + partner * s


def _inproj_body(x_ref, g_ref, w_ref, cos_ref, sin_ref,
                 u_ref, qm_ref, km_ref, vm_ref, kb_ref, vb_ref, qr_ref, kr_ref, vr_ref, gr_ref,
                 gate_ref, ksum_ref, *, transposed_kv):
    hb = _rmsnorm(x_ref[...], g_ref[...]).astype(BF16)
    cos = cos_ref[...]
    sin = sin_ref[...]

    def proj(lo, n):
        return jnp.dot(hb, w_ref[:, lo:lo + n], preferred_element_type=F32)

    u_ref[...] = proj(OFF_U, S5_WIDTH)
    qm_ref[...] = _rope(proj(OFF_QM, MOBA_WIDTH), cos, sin)
    km = _rope(proj(OFF_KM, MOBA_WIDTH), cos, sin)
    kb_ref[...] = km.astype(BF16)
    ksum_ref[0] = jnp.sum(km, axis=0, keepdims=True)
    vm = proj(OFF_VM, MOBA_WIDTH)
    if transposed_kv:
        km, vm = km.T, vm.T
    km_ref[...] = km
    vm_ref[...] = vm
    vb_ref[...] = vm.astype(BF16)
    qr_ref[...] = _rope(proj(OFF_QR, RET_QK_WIDTH), cos, sin)
    kr_ref[...] = _rope(proj(OFF_KR, RET_QK_WIDTH), cos, sin) * (RET_DK ** -0.5)
    vr_ref[...] = proj(OFF_VR, RET_V_WIDTH)
    gr = proj(OFF_GR, RET_V_WIDTH)
    gr_ref[...] = gr * jax.nn.sigmoid(gr)
    for i in range(3):
        gate_ref[:, i * D_MODEL:(i + 1) * D_MODEL] = jax.nn.sigmoid(proj(OFF_GATES + i * D_MODEL, D_MODEL))


def _inproj(x, g, w, cos, sin, seq_tiles=None):
    n = x.shape[0]
    nt = n // ROW_TILE
    row = lambda width: pl.BlockSpec((ROW_TILE, width), lambda i: (i, 0))
    widths = (S5_WIDTH, MOBA_WIDTH, MOBA_WIDTH, MOBA_WIDTH, MOBA_WIDTH, MOBA_WIDTH,
              RET_QK_WIDTH, RET_QK_WIDTH, RET_V_WIDTH, RET_V_WIDTH, 3 * D_MODEL)
    dtypes = (F32, F32, F32, F32, BF16, BF16, F32, F32, F32, F32, F32)
    out_shape = [jax.ShapeDtypeStruct((n, wd), dt) for wd, dt in zip(widths, dtypes)]
    out_shape.append(jax.ShapeDtypeStruct((nt, 1, MOBA_WIDTH), F32))
    out_specs = [row(wd) for wd in widths]
    out_specs.append(pl.BlockSpec((1, 1, MOBA_WIDTH), lambda i: (i, 0, 0)))
    if seq_tiles is not None:
        n_seq = nt // seq_tiles
        kv_t = pl.BlockSpec((None, MOBA_WIDTH, ROW_TILE), lambda i: (i // seq_tiles, 0, i % seq_tiles))
        kv_t_shape = (n_seq, MOBA_WIDTH, seq_tiles * ROW_TILE)
        out_shape[2] = out_shape[3] = jax.ShapeDtypeStruct(kv_t_shape, F32)
        out_specs[2] = out_specs[3] = kv_t
        out_shape[5] = jax.ShapeDtypeStruct((n_seq, seq_tiles, MOBA_WIDTH, ROW_TILE), BF16)
        out_specs[5] = pl.BlockSpec((None, None, MOBA_WIDTH, ROW_TILE),
                                    lambda i: (i // seq_tiles, i % seq_tiles, 0, 0))
    return pl.pallas_call(
        functools.partial(_inproj_body, transposed_kv=seq_tiles is not None),
        out_shape=out_shape,
        grid=(nt,),
        in_specs=[row(D_MODEL), _const_spec((1, D_MODEL)), _const_spec((D_MODEL, IN_COLS)),
                  row(LANES), row(LANES)],
        out_specs=out_specs,
        compiler_params=_params(1),
        name="inproj",
    )(x, g, w, cos, sin)


def _gelu_tanh(x):
    cdf = 0.5 * (1.0 + jnp.tanh(math.sqrt(2.0 / math.pi) * (x + 0.044715 * (x * x * x))))
    return x * cdf


def _s5_body(u_ref, x0r_ref, x0i_ref, lamr_ref, lami_ref, ljr_ref, lji_ref, wbr_ref, wbi_ref,
             wcr_ref, wci_ref, d_ref, wg_ref,
             y_ref, xlr_ref, xli_ref,
             xr_s, xi_s, cr_s, ci_s, carr_s, cari_s, *, steps, chained):
    n_bundles = S5_GROUPS // S5_BUNDLE
    in_w = S5_BUNDLE * S5_GROUP
    st_w = S5_BUNDLE * S5_STATE
    u = u_ref[...]
    ub = u.astype(BF16)
    for j in range(n_bundles):
        uj = ub[:, j * in_w:(j + 1) * in_w]
        xr_s[:, j * st_w:(j + 1) * st_w] = jnp.dot(uj, wbr_ref[j], preferred_element_type=F32)
        xi_s[:, j * st_w:(j + 1) * st_w] = jnp.dot(uj, wbi_ref[j], preferred_element_type=F32)

    def scan(init_r, init_i, lo, store):
        lanes = slice(lo, lo + S5_LANE_GROUP)
        lr = lamr_ref[:, lanes]
        li = lami_ref[:, lanes]

        def step(t, carry):
            xr, xi = carry
            rows = pl.ds(pl.multiple_of(t * SUBLANES, SUBLANES), SUBLANES)
            nr = lr * xr - li * xi + xr_s[rows, lanes]
            ni = lr * xi + li * xr + xi_s[rows, lanes]
            if store:
                xr_s[rows, lanes] = nr
                xi_s[rows, lanes] = ni
            return nr, ni

        return lax.fori_loop(0, steps, step, (init_r, init_i), unroll=4)

    if chained:
        @pl.when(pl.program_id(1) == 0)
        def _():
            carr_s[...] = x0r_ref[...]
            cari_s[...] = x0i_ref[...]

    zeros = jnp.zeros((SUBLANES, S5_LANE_GROUP), F32)
    for lo in range(0, S5_FLAT, S5_LANE_GROUP):
        lanes = slice(lo, lo + S5_LANE_GROUP)
        if chained:
            end_r, end_i = scan(zeros, zeros, lo, store=False)
            c_r = carr_s[:, lanes]
            c_i = cari_s[:, lanes]
            ljr = ljr_ref[:, lanes]
            lji = lji_ref[:, lanes]
            for s in range(SUBLANES):
                cr_s[s:s + 1, lanes] = c_r
                ci_s[s:s + 1, lanes] = c_i
                n_r = ljr * c_r - lji * c_i + end_r[s:s + 1, :]
                n_i = ljr * c_i + lji * c_r + end_i[s:s + 1, :]
                c_r, c_i = n_r, n_i
            carr_s[:, lanes] = c_r
            cari_s[:, lanes] = c_i
            scan(cr_s[:, lanes], ci_s[:, lanes], lo, store=True)
        else:
            f_r, f_i = scan(x0r_ref[:, lanes], x0i_ref[:, lanes], lo, store=True)
            xlr_ref[:, lanes] = f_r
            xli_ref[:, lanes] = f_i
    if chained:
        xlr_ref[...] = carr_s[...]
        xli_ref[...] = cari_s[...]

    ys = []
    for j in range(n_bundles):
        sl = slice(j * st_w, (j + 1) * st_w)
        ys.append(_mm(xr_s[:, sl], wcr_ref[j]) + _mm(xi_s[:, sl], wci_ref[j]))
    y = jnp.concatenate(ys, axis=1) + d_ref[...] * u
    y = _gelu_tanh(y)
    y_ref[...] = y * jax.nn.sigmoid(_mm(y, wg_ref[...]))


def _s5_mixer(u, x0r, x0i, prm, *, steps, chained):
    n_groups, rows, _ = u.shape
    t = SUBLANES * steps
    n_chunks = rows // t

    def swap_row_order(a, inner, outer):
        return a.reshape(n_groups, n_chunks, inner, outer, -1).transpose(0, 1, 3, 2, 4).reshape(a.shape)

    u = swap_row_order(u, SUBLANES, steps)
    r0 = x0r.shape[1]
    chunk = pl.BlockSpec((None, t, S5_WIDTH), lambda b, c: (b, c, 0))
    state = pl.BlockSpec((None, r0, S5_FLAT), lambda b, c: (b, 0, 0))
    n_bundles = S5_GROUPS // S5_BUNDLE
    in_w = S5_BUNDLE * S5_GROUP
    st_w = S5_BUNDLE * S5_STATE
    body = functools.partial(_s5_body, steps=steps, chained=chained)
    y, xl_r, xl_i = pl.pallas_call(
        body,
        out_shape=[jax.ShapeDtypeStruct(u.shape, F32),
                   jax.ShapeDtypeStruct(x0r.shape, F32), jax.ShapeDtypeStruct(x0r.shape, F32)],
        grid=(n_groups, n_chunks),
        in_specs=[chunk, state, state,
                  _const_spec((SUBLANES, S5_FLAT)), _const_spec((SUBLANES, S5_FLAT)),
                  _const_spec((1, S5_FLAT)), _const_spec((1, S5_FLAT)),
                  _const_spec((n_bundles, in_w, st_w)), _const_spec((n_bundles, in_w, st_w)),
                  _const_spec((n_bundles, st_w, in_w)), _const_spec((n_bundles, st_w, in_w)),
                  _const_spec((1, S5_WIDTH)), _const_spec((S5_WIDTH, S5_WIDTH))],
        out_specs=[chunk, state, state],
        scratch_shapes=[pltpu.VMEM((t, S5_FLAT), F32), pltpu.VMEM((t, S5_FLAT), F32),
                        pltpu.VMEM((SUBLANES, S5_FLAT), F32), pltpu.VMEM((SUBLANES, S5_FLAT), F32),
                        pltpu.VMEM((1, S5_FLAT), F32), pltpu.VMEM((1, S5_FLAT), F32)],
        compiler_params=_params(2),
        name="s5_chained" if chained else "s5_independent",
    )(u, x0r, x0i, prm["lam_r8"], prm["lam_i8"], prm["lj_r"][steps], prm["lj_i"][steps],
      prm["wb_r"], prm["wb_i"], prm["wc_r"], prm["wc_i"], prm["d"], prm["w_glu"])
    return swap_row_order(y, steps, SUBLANES), xl_r, xl_i


def _s5_prepare(lam_re, lam_im, log_dt, b_re, b_im, c_re, c_im, d_skip, w_glu, step_counts):
    a = jnp.minimum(lam_re.astype(F32), -1e-4)
    b = lam_im.astype(F32)
    dt = jnp.exp(log_dt.astype(F32))[:, None]

    def lam_bar_power(n):
        mag = jnp.exp(a * dt * n)
        return mag * jnp.cos(b * dt * n), mag * jnp.sin(b * dt * n)

    e_r, e_i = lam_bar_power(1.0)
    den = a * a + b * b
    cf_r = ((e_r - 1.0) * a + e_i * b) / den
    cf_i = (e_i * a - (e_r - 1.0) * b) / den
    bb_r = cf_r[..., None] * b_re.astype(F32) - cf_i[..., None] * b_im.astype(F32)
    bb_i = cf_r[..., None] * b_im.astype(F32) + cf_i[..., None] * b_re.astype(F32)
    n_bundles = S5_GROUPS // S5_BUNDLE
    eye = jnp.eye(S5_BUNDLE, dtype=F32)

    def expand_b(t):
        t = t.reshape(n_bundles, S5_BUNDLE, S5_STATE, S5_GROUP)
        w = jnp.einsum("jgpc,gh->jgchp", t, eye)
        return w.reshape(n_bundles, S5_BUNDLE * S5_GROUP, S5_BUNDLE * S5_STATE).astype(BF16)

    def expand_c(t):
        t = t.reshape(n_bundles, S5_BUNDLE, S5_GROUP, S5_STATE)
        w = jnp.einsum("jgcp,gh->jgphc", t, eye)
        return w.reshape(n_bundles, S5_BUNDLE * S5_STATE, S5_BUNDLE * S5_GROUP).astype(BF16)

    flat = lambda t: t.reshape(1, S5_FLAT)
    lj_r, lj_i = {}, {}
    for steps in step_counts:
        p_r, p_i = lam_bar_power(float(steps))
        lj_r[steps] = flat(p_r)
        lj_i[steps] = flat(p_i)
    return {
        "lam_r8": jnp.broadcast_to(flat(e_r), (SUBLANES, S5_FLAT)),
        "lam_i8": jnp.broadcast_to(flat(e_i), (SUBLANES, S5_FLAT)),
        "lj_r": lj_r, "lj_i": lj_i,
        "wb_r": expand_b(bb_r), "wb_i": expand_b(bb_i),
        "wc_r": expand_c(c_re.astype(F32)), "wc_i": expand_c(-c_im.astype(F32)),
        "d": d_skip.astype(F32).reshape(1, S5_WIDTH),
        "w_glu": w_glu.astype(BF16),
    }


def _retention_body(q_ref, k_ref, v_ref, g_ref, s0_ref, dm_ref, qd_ref, kd_ref, cd_ref,
                    y_ref, sout_ref, s_s):
    @pl.when(pl.program_id(1) == 0)
    def _():
        s_s[...] = s0_ref[...]

    for h in range(RET_HEADS):
        qk = slice(h * RET_DK, (h + 1) * RET_DK)
        vv = slice(h * RET_DV, (h + 1) * RET_DV)
        q = q_ref[:, qk].astype(BF16)
        k = k_ref[:, qk]
        v = v_ref[:, vv].astype(BF16)
        s = s_s[h]
        inner = lax.dot_general(q, k.astype(BF16), NT_DIMS, preferred_element_type=F32) * dm_ref[h]
        o = (jnp.dot(inner.astype(BF16), v, preferred_element_type=F32)
             + jnp.dot(q, s.astype(BF16), preferred_element_type=F32) * qd_ref[h])
        k_dec = (k * kd_ref[h]).astype(BF16)
        s_s[h] = s * cd_ref[h] + lax.dot_general(k_dec, v, TN_DIMS, preferred_element_type=F32)
        mu = jnp.mean(o, axis=-1, keepdims=True)
        var = jnp.mean(jnp.square(o - mu), axis=-1, keepdims=True)
        y_ref[:, vv] = g_ref[:, vv] * ((o - mu) * lax.rsqrt(var + NORM_EPS))
    sout_ref[...] = s_s[...]


def _retention(q, k, v, g, s0, chunk):
    bsz, seq_len, _ = q.shape
    n_chunks = seq_len // chunk
    log_g = jnp.log(1.0 - 2.0 ** (-5.0 - jnp.arange(RET_HEADS, dtype=F32)))
    idx = jnp.arange(chunk, dtype=F32)
    diff = idx[:, None] - idx[None, :]
    decay_mask = jnp.where(diff >= 0, jnp.exp(log_g[:, None, None] * jnp.maximum(diff, 0.0)), 0.0)
    q_decay = jnp.exp(log_g[:, None] * (idx + 1.0))[:, :, None]
    k_decay = jnp.exp(log_g[:, None] * (chunk - 1.0 - idx))[:, :, None]
    chunk_decay = jnp.broadcast_to(jnp.exp(log_g * chunk)[:, None, None], (RET_HEADS, 1, RET_DV))
    rows = lambda width: pl.BlockSpec((None, chunk, width), lambda b, c: (b, c, 0))
    state = pl.BlockSpec((None, RET_HEADS, RET_DK, RET_DV), lambda b, c: (b, 0, 0, 0))
    return pl.pallas_call(
        _retention_body,
        out_shape=[jax.ShapeDtypeStruct(v.shape, F32), jax.ShapeDtypeStruct(s0.shape, F32)],
        grid=(bsz, n_chunks),
        in_specs=[rows(RET_QK_WIDTH), rows(RET_QK_WIDTH), rows(RET_V_WIDTH), rows(RET_V_WIDTH), state,
                  _const_spec((RET_HEADS, chunk, chunk)), _const_spec((RET_HEADS, chunk, 1)),
                  _const_spec((RET_HEADS, chunk, 1)), _const_spec((RET_HEADS, 1, RET_DV))],
        out_specs=[rows(RET_V_WIDTH), state],
        scratch_shapes=[pltpu.VMEM((RET_HEADS, RET_DK, RET_DV), F32)],
        compiler_params=_params(2),
        name=f"retention_c{chunk}",
    )(q, k, v, g, s0.astype(F32), decay_mask, q_decay, k_decay, chunk_decay)


def _topk_rank(gate, axis, n):
    idx = lax.broadcasted_iota(jnp.int32, gate.shape, axis)
    rank = jnp.zeros(gate.shape, jnp.int32)
    for other in range(n):
        g_o = gate[:, other:other + 1] if axis == 1 else gate[other:other + 1, :]
        beats = (g_o > gate) | ((g_o == gate) & (other < idx))
        rank = rank + beats.astype(jnp.int32)
    return rank


def _moba_prompt_body(q_ref, k_ref, vt_ref, ksum_ref, o_ref, w_s, sel_s, s_s, p_s, acc_s, *, n_blocks):
    qb = pl.program_id(1)
    own = (qb * MOBA_QBLOCK) // MOBA_BLOCK
    q_off = (qb * MOBA_QBLOCK) % MOBA_BLOCK
    key_i = lax.broadcasted_iota(jnp.int32, (MOBA_BLOCK, MOBA_QBLOCK), 0)
    q_i = lax.broadcasted_iota(jnp.int32, (MOBA_BLOCK, MOBA_QBLOCK), 1)
    causal = key_i <= q_off + q_i
    blk = lax.broadcasted_iota(jnp.int32, (n_blocks, MOBA_QBLOCK), 0)
    scale = MOBA_HEAD_DIM ** -0.5
    q_t = q_ref[...].T
    zero_half = jnp.zeros((MOBA_HEAD_DIM, MOBA_QBLOCK), F32)
    for h in range(MOBA_HEADS):
        hd = slice(h * MOBA_HEAD_DIM, (h + 1) * MOBA_HEAD_DIM)
        q_h = q_t[hd, :]
        kmean = ksum_ref[:, hd] * (1.0 / MOBA_BLOCK)
        gate = jnp.dot(kmean, q_h, precision=lax.Precision.HIGHEST, preferred_element_type=F32)
        gate = jnp.where(blk < own, gate, NEG_INF)
        sel_s[h] = ((_topk_rank(gate, 0, n_blocks) < MOBA_TOPK) & (blk < own)).astype(F32)
        halves = [q_h * scale, zero_half] if h % 2 == 0 else [zero_half, q_h * scale]
        w_s[h] = jnp.concatenate(halves, axis=0).astype(BF16)
    heads = range(MOBA_HEADS)
    hd_of = lambda h: slice(h * MOBA_HEAD_DIM, (h + 1) * MOBA_HEAD_DIM)

    def scores(slot, n):
        rows = pl.ds(pl.multiple_of(n * MOBA_BLOCK, MOBA_BLOCK), MOBA_BLOCK)
        for h in heads:
            pair = slice((h // 2) * LANES, (h // 2 + 1) * LANES)
            s_s[slot, h] = jnp.dot(k_ref[rows, pair], w_s[h], preferred_element_type=F32)

    scores(0, own)
    m, l = [], []
    for h in heads:
        s = jnp.where(causal, s_s[0, h], NEG_INF)
        m_h = jnp.max(s, axis=0, keepdims=True)
        p = jnp.exp(s - m_h)
        p_s[h, 0:MOBA_BLOCK, :] = p.astype(BF16)
        m.append(m_h)
        l.append(jnp.sum(p, axis=0, keepdims=True))
    for h in heads:
        acc_s[hd_of(h), :] = jnp.dot(vt_ref[own, hd_of(h), :], p_s[h, 0:MOBA_BLOCK, :],
                                     preferred_element_type=F32)

    def past_pair(t, carry):
        m, l = carry
        blocks = (2 * t, jnp.minimum(2 * t + 1, n_blocks - 1))
        valid = (True, 2 * t + 1 < own)
        for slot, n in enumerate(blocks):
            scores(slot, n)
        m_out, l_out, alphas = [], [], []
        for h in heads:
            chosen = [(sel_s[h, pl.ds(n, 1), :] > 0.5) & ok for n, ok in zip(blocks, valid)]
            tops = [jnp.where(c, jnp.max(s_s[slot, h], axis=0, keepdims=True), NEG_INF)
                    for slot, c in enumerate(chosen)]
            m_new = jnp.maximum(m[h], jnp.maximum(tops[0], tops[1]))
            alpha = jnp.exp(m[h] - m_new)
            l_new = alpha * l[h]
            for slot, c in enumerate(chosen):
                p = jnp.exp(s_s[slot, h] - jnp.where(c, m_new, -NEG_INF))
                p_s[h, slot * MOBA_BLOCK:(slot + 1) * MOBA_BLOCK, :] = p.astype(BF16)
                l_new = l_new + jnp.sum(p, axis=0, keepdims=True)
            m_out.append(m_new)
            l_out.append(l_new)
            alphas.append(alpha)
        for h in heads:
            vt = jnp.concatenate([vt_ref[n, hd_of(h), :] for n in blocks], axis=1)
            acc_s[hd_of(h), :] = alphas[h] * acc_s[hd_of(h), :] + jnp.dot(vt, p_s[h],
                                                                        preferred_element_type=F32)
        return tuple(m_out), tuple(l_out)

    m, l = lax.fori_loop(0, (own + 1) // 2, past_pair, (tuple(m), tuple(l)))
    o_ref[...] = jnp.concatenate([acc_s[hd_of(h), :] / l[h] for h in heads], axis=0).T


def _moba_prompt(q, kb, vt, ksum):
    bsz, seq_len, _ = q.shape
    n_blocks = seq_len // MOBA_BLOCK
    qblock = pl.BlockSpec((None, MOBA_QBLOCK, MOBA_WIDTH), lambda b, i: (b, i, 0))
    return pl.pallas_call(
        functools.partial(_moba_prompt_body, n_blocks=n_blocks),
        out_shape=jax.ShapeDtypeStruct(q.shape, F32),
        grid=(bsz, seq_len // MOBA_QBLOCK),
        in_specs=[qblock,
                  pl.BlockSpec((None, seq_len, MOBA_WIDTH), lambda b, i: (b, 0, 0)),
                  pl.BlockSpec((None, n_blocks, MOBA_WIDTH, MOBA_BLOCK), lambda b, i: (b, 0, 0, 0)),
                  pl.BlockSpec((None, n_blocks, MOBA_WIDTH), lambda b, i: (b, 0, 0))],
        out_specs=qblock,
        scratch_shapes=[pltpu.VMEM((MOBA_HEADS, LANES, MOBA_QBLOCK), BF16),
                        pltpu.VMEM((MOBA_HEADS, n_blocks, MOBA_QBLOCK), F32),
                        pltpu.VMEM((2, MOBA_HEADS, MOBA_BLOCK, MOBA_QBLOCK), F32),
                        pltpu.VMEM((MOBA_HEADS, 2 * MOBA_BLOCK, MOBA_QBLOCK), BF16),
                        pltpu.VMEM((MOBA_WIDTH, MOBA_QBLOCK), F32)],
        compiler_params=_params(2),
        name="moba_prompt",
    )(q, kb, vt, ksum)


def _moba_sample_body(pt_ref, q_ref, kn_ref, vn_ref, *rest, n_pages, dec_seq):
    del pt_ref
    pages_per_block = MOBA_BLOCK // PAGE_SIZE
    pps = SAMPLE_BLOCKS_PER_STEP * pages_per_block
    ck_refs = rest[:pps]
    cv_refs = rest[pps:2 * pps]
    o_ref, qbd_s, qs_s, kmt_s, m_s, l_s, acc_s, s_s, p_s = rest[2 * pps:]
    step = pl.program_id(1)
    n_steps = n_pages // pps
    n_blocks = n_pages // pages_per_block
    n_cols = MOBA_HEADS * dec_seq
    scale = MOBA_HEAD_DIM ** -0.5
    r_i = lax.broadcasted_iota(jnp.int32, (n_cols, MOBA_WIDTH), 0)
    c_i = lax.broadcasted_iota(jnp.int32, (n_cols, MOBA_WIDTH), 1)
    head_diag = (r_i // dec_seq) == (c_i // MOBA_HEAD_DIM)

    @pl.when(step == 0)
    def _():
        q_rep = jnp.concatenate([q_ref[...]] * MOBA_HEADS, axis=0)
        q_bd = jnp.where(head_diag, q_rep, 0.0)
        qbd_s[...] = q_bd
        qs_s[...] = (q_bd * scale).astype(BF16)
        m_s[...] = jnp.full(m_s.shape, NEG_INF, F32)
        l_s[...] = jnp.zeros(l_s.shape, F32)

    qs = qs_s[...]
    blk_lane = lax.broadcasted_iota(jnp.int32, (n_cols, n_blocks), 1)
    kmt_lane = lax.broadcasted_iota(jnp.int32, (MOBA_WIDTH, n_blocks), 1)
    blocks = range(SAMPLE_BLOCKS_PER_STEP)
    page_of = lambda refs, j: jnp.concatenate(
        [refs[j * pages_per_block + i][...].reshape(MOBA_WIDTH, PAGE_SIZE) for i in range(pages_per_block)],
        axis=1)
    kmeans, tops, sums = [], [], []
    for j in blocks:
        kt = page_of(ck_refs, j)
        kmeans.append(jnp.sum(kt, axis=1, keepdims=True) * (1.0 / MOBA_BLOCK))
        s_s[j] = jnp.dot(qs, kt.astype(BF16), preferred_element_type=F32)
    for j in blocks:
        s = s_s[j]
        m_n = jnp.max(s, axis=1, keepdims=True)
        p = jnp.exp(s - m_n)
        p_s[j] = p.astype(BF16)
        tops.append(m_n)
        sums.append(jnp.sum(p, axis=1, keepdims=True))
    for j in blocks:
        acc_s[step * SAMPLE_BLOCKS_PER_STEP + j] = lax.dot_general(
            p_s[j], page_of(cv_refs, j).astype(BF16), NT_DIMS, preferred_element_type=F32)
    kmt, m_all, l_all = kmt_s[...], m_s[...], l_s[...]
    for j in range(SAMPLE_BLOCKS_PER_STEP):
        block = step * SAMPLE_BLOCKS_PER_STEP + j
        kmt = jnp.where(kmt_lane == block, kmeans[j], kmt)
        m_all = jnp.where(blk_lane == block, tops[j], m_all)
        l_all = jnp.where(blk_lane == block, sums[j], l_all)
    kmt_s[...] = kmt
    m_s[...] = m_all
    l_s[...] = l_all

    @pl.when(step == n_steps - 1)
    def _():
        gate = jnp.dot(qbd_s[...], kmt_s[...], precision=lax.Precision.HIGHEST,
                       preferred_element_type=F32)
        sel = _topk_rank(gate, 1, n_blocks) < MOBA_TOPK
        s_own = lax.dot_general(qs_s[...], kn_ref[...].astype(BF16), NT_DIMS, preferred_element_type=F32)
        o_row = lax.broadcasted_iota(jnp.int32, s_own.shape, 0)
        o_lane = lax.broadcasted_iota(jnp.int32, s_own.shape, 1)
        s_own = jnp.where(o_lane <= (o_row % dec_seq), s_own, NEG_INF)
        m_own = jnp.max(s_own, axis=1, keepdims=True)
        p_own = jnp.exp(s_own - m_own)
        l_own = jnp.sum(p_own, axis=1, keepdims=True)
        acc_own = jnp.dot(p_own.astype(BF16), vn_ref[...].astype(BF16), preferred_element_type=F32)
        m_all = m_s[...]
        top = jnp.maximum(m_own, jnp.max(jnp.where(sel, m_all, NEG_INF), axis=1, keepdims=True))
        w = jnp.exp(jnp.where(sel, m_all - top, NEG_INF))
        w_own = jnp.exp(m_own - top)
        denom = w_own * l_own + jnp.sum(w * l_s[...], axis=1, keepdims=True)
        out = w_own * acc_own
        for n in range(n_blocks):
            out = out + w[:, n:n + 1] * acc_s[n]
        out = jnp.where(head_diag, out / denom, 0.0)
        res = out[0:dec_seq]
        for h in range(1, MOBA_HEADS):
            res = res + out[h * dec_seq:(h + 1) * dec_seq]
        o_ref[...] = res


def _moba_sample(layer, page_table, q, k_new, v_new, cache_kt, cache_vt):
    bsz, dec_seq, _ = q.shape
    n_pages = page_table.shape[1]
    pages_per_block = MOBA_BLOCK // PAGE_SIZE
    pps = SAMPLE_BLOCKS_PER_STEP * pages_per_block
    n_steps = n_pages // pps
    n_blocks = n_pages // pages_per_block
    n_cols = MOBA_HEADS * dec_seq
    tok = pl.BlockSpec((None, dec_seq, MOBA_WIDTH), lambda b, s, pt: (b, 0, 0))

    def page_spec(i):
        return pl.BlockSpec((None, None, MOBA_HEADS, MOBA_HEAD_DIM, PAGE_SIZE),
                            lambda b, s, pt: (layer, pt[b, s * pps + i], 0, 0, 0))

    grid_spec = pltpu.PrefetchScalarGridSpec(
        num_scalar_prefetch=1,
        grid=(bsz, n_steps),
        in_specs=[tok, tok, tok] + [page_spec(i) for i in range(pps)] * 2,
        out_specs=tok,
        scratch_shapes=[pltpu.VMEM((n_cols, MOBA_WIDTH), F32),
                        pltpu.VMEM((n_cols, MOBA_WIDTH), BF16),
                        pltpu.VMEM((MOBA_WIDTH, n_blocks), F32),
                        pltpu.VMEM((n_cols, n_blocks), F32),
                        pltpu.VMEM((n_cols, n_blocks), F32),
                        pltpu.VMEM((n_blocks, n_cols, MOBA_WIDTH), F32),
                        pltpu.VMEM((SAMPLE_BLOCKS_PER_STEP, n_cols, MOBA_BLOCK), F32),
                        pltpu.VMEM((SAMPLE_BLOCKS_PER_STEP, n_cols, MOBA_BLOCK), BF16)],
    )
    return pl.pallas_call(
        functools.partial(_moba_sample_body, n_pages=n_pages, dec_seq=dec_seq),
        out_shape=jax.ShapeDtypeStruct(q.shape, F32),
        grid_spec=grid_spec,
        compiler_params=_params(2),
        name="moba_sample",
    )(page_table, q, k_new, v_new, *([cache_kt] * pps), *([cache_vt] * pps))


def _merge_body(x_ref, ys_ref, ym_ref, yr_ref, gate_ref, wbs_ref, wbm_ref, wbr_ref, wo_ref, g2_ref,
                wu_ref, wd_ref, gf_ref, o_ref, *, final):
    merged = (gate_ref[:, 0:D_MODEL] * _mm(ys_ref[...], wbs_ref[...])
              + gate_ref[:, D_MODEL:2 * D_MODEL] * _mm(ym_ref[...], wbm_ref[...])
              + gate_ref[:, 2 * D_MODEL:3 * D_MODEL] * _mm(yr_ref[...], wbr_ref[...]))
    x = x_ref[...] + _mm(merged, wo_ref[...])
    up = _mm(_rmsnorm(x, g2_ref[...]), wu_ref[...])
    x = x + _mm(jnp.square(jnp.maximum(up, 0.0)), wd_ref[...])
    if final:
        x = _rmsnorm(x, gf_ref[...])
    o_ref[...] = x


def _merge(x, y_s5, y_moba, y_ret, gates, w, final):
    n = x.shape[0]
    row = lambda width: pl.BlockSpec((ROW_TILE, width), lambda i: (i, 0))
    return pl.pallas_call(
        functools.partial(_merge_body, final=final),
        out_shape=jax.ShapeDtypeStruct(x.shape, F32),
        grid=(n // ROW_TILE,),
        in_specs=[row(D_MODEL), row(S5_WIDTH), row(MOBA_WIDTH), row(RET_V_WIDTH), row(3 * D_MODEL),
                  _const_spec((S5_WIDTH, D_MODEL)), _const_spec((MOBA_WIDTH, D_MODEL)),
                  _const_spec((RET_V_WIDTH, D_MODEL)), _const_spec((D_MODEL, D_MODEL)),
                  _const_spec((1, D_MODEL)), _const_spec((D_MODEL, D_FF)), _const_spec((D_FF, D_MODEL)),
                  _const_spec((1, D_MODEL))],
        out_specs=row(D_MODEL),
        compiler_params=_params(1),
        name="merge_mlp",
    )(x, y_s5, y_moba, y_ret, gates, w["w_br_s5"], w["w_br_moba"], w["w_br_ret"], w["w_out"],
      w["norm2_g"], w["w_mlp_up"], w["w_mlp_down"], w["final_g"])


def _rope_tables(pos):
    half = MOBA_HEAD_DIM // 2
    freqs = ROPE_THETA ** (-jnp.arange(half, dtype=F32) / half)
    ang = pos.astype(F32)[:, None] * freqs[None, :]
    cos = jnp.cos(ang)
    sin = jnp.sin(ang)
    reps = LANES // MOBA_HEAD_DIM
    return (jnp.concatenate([cos, cos] * reps, axis=1), jnp.concatenate([-sin, sin] * reps, axis=1))


def kernel(x_prompt, x_sample, cache_k, cache_v, state_s5_re, state_s5_im, state_ret, page_table,
           norm1_g, w_in, s5_lambda_re, s5_lambda_im, s5_log_dt, s5_b_re, s5_b_im, s5_c_re, s5_c_im,
           s5_d, s5_w_glu, w_br_s5, w_br_moba, w_br_ret, w_out, norm2_g, w_mlp_up, w_mlp_down, final_g):
    bsz_p, seq_p, _ = x_prompt.shape
    bsz_s, seq_s, _ = x_sample.shape
    depth = w_in.shape[0]
    n_pages = page_table.shape[1]
    past_len = n_pages * PAGE_SIZE
    n_p = bsz_p * seq_p
    n_s = bsz_s * seq_s
    prompt_steps = ROW_TILE // SUBLANES
    assert seq_p % ROW_TILE == 0 and n_s % ROW_TILE == 0 and seq_p % RET_CHUNK == 0
    assert ROW_TILE == MOBA_BLOCK
    assert n_pages % (SAMPLE_BLOCKS_PER_STEP * MOBA_BLOCK // PAGE_SIZE) == 0
    assert seq_s == SUBLANES and bsz_s % SUBLANES == 0 and seq_s % RET_CHUNK != 0

    cos_p, sin_p = _rope_tables(jnp.tile(jnp.arange(seq_p, dtype=jnp.int32), bsz_p))
    cos_s, sin_s = _rope_tables(jnp.tile(past_len + jnp.arange(seq_s, dtype=jnp.int32), bsz_s))
    zeros_s5 = jnp.zeros((bsz_p, 1, S5_FLAT), F32)
    zeros_ret = jnp.zeros((bsz_p, RET_HEADS, RET_DK, RET_DV), F32)
    cache_kt = cache_k.transpose(0, 1, 3, 4, 2)
    cache_vt = cache_v.transpose(0, 1, 3, 4, 2)

    hp = x_prompt.reshape(n_p, D_MODEL)
    hs = x_sample.reshape(n_s, D_MODEL)
    outs = {name: [] for name in ("kp", "vp", "ks", "vs", "s5rp", "s5ip", "s5rs", "s5is", "retp", "rets")}
    for l in range(depth):
        last = l == depth - 1
        w = {"w_br_s5": w_br_s5[l].astype(BF16), "w_br_moba": w_br_moba[l].astype(BF16),
             "w_br_ret": w_br_ret[l].astype(BF16), "w_out": w_out[l].astype(BF16),
             "norm2_g": norm2_g[l].reshape(1, D_MODEL), "w_mlp_up": w_mlp_up[l].astype(BF16),
             "w_mlp_down": w_mlp_down[l].astype(BF16), "final_g": final_g.reshape(1, D_MODEL)}
        g1 = norm1_g[l].reshape(1, D_MODEL)
        w_in_l = w_in[l].astype(BF16)
        s5p = _s5_prepare(s5_lambda_re[l], s5_lambda_im[l], s5_log_dt[l], s5_b_re[l], s5_b_im[l],
                          s5_c_re[l], s5_c_im[l], s5_d[l], s5_w_glu[l], (prompt_steps, seq_s))

        n_blk = seq_p // MOBA_BLOCK
        (u, qm, km_t, vm_t, kb, vt, qr, kr, vr, gr, gates, ksum) = _inproj(hp, g1, w_in_l, cos_p, sin_p,
                                                                           seq_tiles=n_blk)
        seq = lambda t: t.reshape(bsz_p, seq_p, t.shape[-1])
        y_s5, s5r, s5i = _s5_mixer(seq(u), zeros_s5, zeros_s5, s5p, steps=prompt_steps, chained=True)
        y_moba = _moba_prompt(seq(qm), seq(kb), vt, ksum.reshape(bsz_p, n_blk, MOBA_WIDTH))
        y_ret, ret_s = _retention(seq(qr), seq(kr), seq(vr), seq(gr), zeros_ret, RET_CHUNK)
        hp = _merge(hp, y_s5.reshape(n_p, -1), y_moba.reshape(n_p, -1), y_ret.reshape(n_p, -1), gates, w, last)
        per_head = lambda t: t.reshape(bsz_p, MOBA_HEADS, MOBA_HEAD_DIM, seq_p).transpose(0, 3, 1, 2)
        outs["kp"].append(per_head(km_t))
        outs["vp"].append(per_head(vm_t))
        outs["s5rp"].append(s5r.reshape(bsz_p, S5_GROUPS, S5_STATE))
        outs["s5ip"].append(s5i.reshape(bsz_p, S5_GROUPS, S5_STATE))
        outs["retp"].append(ret_s)

        (u, qm, km, vm, kb, vb, qr, kr, vr, gr, gates, ksum) = _inproj(hs, g1, w_in_l, cos_s, sin_s)
        grp = lambda t: t.reshape(bsz_s // SUBLANES, SUBLANES * seq_s, t.shape[-1])
        st = lambda t: t.astype(F32).reshape(bsz_s // SUBLANES, SUBLANES, S5_FLAT)
        y_s5, s5r, s5i = _s5_mixer(grp(u), st(state_s5_re[l]), st(state_s5_im[l]), s5p,
                                   steps=seq_s, chained=False)
        seq = lambda t: t.reshape(bsz_s, seq_s, t.shape[-1])
        y_moba = _moba_sample(l, page_table, seq(qm), seq(km), seq(vm), cache_kt, cache_vt)
        y_ret, ret_s = _retention(seq(qr), seq(kr), seq(vr), seq(gr), state_ret[l], seq_s)
        hs = _merge(hs, y_s5.reshape(n_s, -1), y_moba.reshape(n_s, -1), y_ret.reshape(n_s, -1), gates, w, last)
        outs["ks"].append(km.reshape(bsz_s, seq_s, MOBA_HEADS, MOBA_HEAD_DIM))
        outs["vs"].append(vm.reshape(bsz_s, seq_s, MOBA_HEADS, MOBA_HEAD_DIM))
        outs["s5rs"].append(s5r.reshape(bsz_s, S5_GROUPS, S5_STATE))
        outs["s5is"].append(s5i.reshape(bsz_s, S5_GROUPS, S5_STATE))
        outs["rets"].append(ret_s)

    stack = lambda name: jnp.stack(outs[name])
    return (hp.reshape(x_prompt.shape), hs.reshape(x_sample.shape),
            stack("kp"), stack("vp"), stack("ks"), stack("vs"),
            stack("s5rp"), stack("s5ip"), stack("s5rs"), stack("s5is"), stack("retp"), stack("rets"))
```

```python
import functools
import math

import jax
import jax.numpy as jnp
from jax import lax
from jax.experimental import pallas as pl
from jax.experimental.pallas import tpu as pltpu

F32 = jnp.float32
BF16 = jnp.bfloat16

D_MODEL = 1024
PAGE_SIZE = 128
S5_WIDTH = 512
S5_GROUP = 16
S5_GROUPS = 32
S5_STATE = 64
S5_FLAT = S5_GROUPS * S5_STATE
MOBA_HEADS = 8
MOBA_HEAD_DIM = 64
MOBA_WIDTH = 512
MOBA_BLOCK = 256
MOBA_TOPK = 3
RET_HEADS = 4
RET_DK = 64
RET_DV = 128
RET_QK_WIDTH = 256
RET_V_WIDTH = 512
RET_CHUNK = 128
D_FF = 4096
ROPE_THETA = 10000.0
NORM_EPS = 1e-6
NEG_INF = -1e30
IN_SPLITS = (S5_WIDTH, MOBA_WIDTH, MOBA_WIDTH, MOBA_WIDTH, RET_QK_WIDTH, RET_QK_WIDTH,
             RET_V_WIDTH, RET_V_WIDTH, D_MODEL, D_MODEL, D_MODEL)
IN_COLS = sum(IN_SPLITS)
(OFF_U, OFF_QM, OFF_KM, OFF_VM, OFF_QR, OFF_KR, OFF_VR, OFF_GR, OFF_GATES) = (
    0, 512, 1024, 1536, 2048, 2304, 2560, 3072, 3584)

LANES = 128
SUBLANES = 8
ROW_TILE = 256
S5_BUNDLE = 8
S5_LANE_GROUP = 512
SAMPLE_BLOCKS_PER_STEP = 8
PROMPT_Q_TILE = 256
VMEM_LIMIT = 56 * 1024 * 1024

NT_DIMS = (((1,), (1,)), ((), ()))
TN_DIMS = (((0,), (0,)), ((), ()))


def _const_spec(shape):
    nd = len(shape)
    return pl.BlockSpec(shape, lambda *_: (0,) * nd, pipeline_mode=pl.Buffered(1))


def _params(n_axes):
    return pltpu.CompilerParams(dimension_semantics=("arbitrary",) * n_axes,
                                vmem_limit_bytes=VMEM_LIMIT)


def _rmsnorm(x, g):
    return x * lax.rsqrt(jnp.mean(x * x, axis=-1, keepdims=True) + NORM_EPS) * g


def _mm(a, w):
    return jnp.dot(a.astype(BF16), w, preferred_element_type=F32)


def _rope(t, cos, sin_signed):
    n = t.shape[1]
    reps = n // LANES
    c = jnp.concatenate([cos] * reps, axis=1)
    s = jnp.concatenate([sin_signed] * reps, axis=1)
    lane = lax.broadcasted_iota(jnp.int32, t.shape, 1)
    first_half = (lane % MOBA_HEAD_DIM) < (MOBA_HEAD_DIM // 2)
    partner = jnp.where(first_half, pltpu.roll(t, n - MOBA_HEAD_DIM // 2, 1),
                        pltpu.roll(t, MOBA_HEAD_DIM // 2, 1))
    return t * c + partner * s


def _inproj_body(x_ref, g_ref, w_ref, cos_ref, sin_ref,
                 u_ref, qm_ref, km_ref, vm_ref, kb_ref, vb_ref, qr_ref, kr_ref, vr_ref, gr_ref,
                 gate_ref, ksum_ref, *, transposed_kv):
    hb = _rmsnorm(x_ref[...], g_ref[...]).astype(BF16)
    cos = cos_ref[...]
    sin = sin_ref[...]

    def proj(lo, n):
        return jnp.dot(hb, w_ref[:, lo:lo + n], preferred_element_type=F32)

    u_ref[...] = proj(OFF_U, S5_WIDTH)
    qm_ref[...] = _rope(proj(OFF_QM, MOBA_WIDTH), cos, sin)
    km = _rope(proj(OFF_KM, MOBA_WIDTH), cos, sin)
    kb_ref[...] = km.astype(BF16)
    ksum_ref[0] = jnp.sum(km, axis=0, keepdims=True)
    vm = proj(OFF_VM, MOBA_WIDTH)
    if transposed_kv:
        km, vm = km.T, vm.T
    km_ref[...] = km
    vm_ref[...] = vm
    vb_ref[...] = vm.astype(BF16)
    qr_ref[...] = _rope(proj(OFF_QR, RET_QK_WIDTH), cos, sin)
    kr_ref[...] = _rope(proj(OFF_KR, RET_QK_WIDTH), cos, sin) * (RET_DK ** -0.5)
    vr_ref[...] = proj(OFF_VR, RET_V_WIDTH)
    gr = proj(OFF_GR, RET_V_WIDTH)
    gr_ref[...] = gr * jax.nn.sigmoid(gr)
    for i in range(3):
        gate_ref[:, i * D_MODEL:(i + 1) * D_MODEL] = jax.nn.sigmoid(proj(OFF_GATES + i * D_MODEL, D_MODEL))


def _inproj(x, g, w, cos, sin, seq_tiles=None):
    n = x.shape[0]
    nt = n // ROW_TILE
    row = lambda width: pl.BlockSpec((ROW_TILE, width), lambda i: (i, 0))
    widths = (S5_WIDTH, MOBA_WIDTH, MOBA_WIDTH, MOBA_WIDTH, MOBA_WIDTH, MOBA_WIDTH,
              RET_QK_WIDTH, RET_QK_WIDTH, RET_V_WIDTH, RET_V_WIDTH, 3 * D_MODEL)
    dtypes = (F32, F32, F32, F32, BF16, BF16, F32, F32, F32, F32, F32)
    out_shape = [jax.ShapeDtypeStruct((n, wd), dt) for wd, dt in zip(widths, dtypes)]
    out_shape.append(jax.ShapeDtypeStruct((nt, 1, MOBA_WIDTH), F32))
    out_specs = [row(wd) for wd in widths]
    out_specs.append(pl.BlockSpec((1, 1, MOBA_WIDTH), lambda i: (i, 0, 0)))
    if seq_tiles is not None:
        n_seq = nt // seq_tiles
        kv_t = pl.BlockSpec((None, MOBA_WIDTH, ROW_TILE), lambda i: (i // seq_tiles, 0, i % seq_tiles))
        kv_t_shape = (n_seq, MOBA_WIDTH, seq_tiles * ROW_TILE)
        out_shape[2] = out_shape[3] = jax.ShapeDtypeStruct(kv_t_shape, F32)
        out_specs[2] = out_specs[3] = kv_t
        out_shape[5] = jax.ShapeDtypeStruct((n_seq, seq_tiles, MOBA_WIDTH, ROW_TILE), BF16)
        out_specs[5] = pl.BlockSpec((None, None, MOBA_WIDTH, ROW_TILE),
                                    lambda i: (i // seq_tiles, i % seq_tiles, 0, 0))
    return pl.pallas_call(
        functools.partial(_inproj_body, transposed_kv=seq_tiles is not None),
        out_shape=out_shape,
        grid=(nt,),
        in_specs=[row(D_MODEL), _const_spec((1, D_MODEL)), _const_spec((D_MODEL, IN_COLS)),
                  row(LANES), row(LANES)],
        out_specs=out_specs,
        compiler_params=_params(1),
        name="inproj",
    )(x, g, w, cos, sin)


def _gelu_tanh(x):
    cdf = 0.5 * (1.0 + jnp.tanh(math.sqrt(2.0 / math.pi) * (x + 0.044715 * (x * x * x))))
    return x * cdf


def _s5_body(u_ref, x0r_ref, x0i_ref, lamr_ref, lami_ref, ljr_ref, lji_ref, wbr_ref, wbi_ref,
             wcr_ref, wci_ref, d_ref, wg_ref,
             y_ref, xlr_ref, xli_ref,
             xr_s, xi_s, cr_s, ci_s, carr_s, cari_s, *, steps, chained):
    n_bundles = S5_GROUPS // S5_BUNDLE
    in_w = S5_BUNDLE * S5_GROUP
    st_w = S5_BUNDLE * S5_STATE
    u = u_ref[...]
    ub = u.astype(BF16)
    for j in range(n_bundles):
        uj = ub[:, j * in_w:(j + 1) * in_w]
        xr_s[:, j * st_w:(j + 1) * st_w] = jnp.dot(uj, wbr_ref[j], preferred_element_type=F32)
        xi_s[:, j * st_w:(j + 1) * st_w] = jnp.dot(uj, wbi_ref[j], preferred_element_type=F32)

    def scan(init_r, init_i, lo, store):
        lanes = slice(lo, lo + S5_LANE_GROUP)
        lr = lamr_ref[:, lanes]
        li = lami_ref[:, lanes]

        def step(t, carry):
            xr, xi = carry
            rows = pl.ds(pl.multiple_of(t * SUBLANES, SUBLANES), SUBLANES)
            nr = lr * xr - li * xi + xr_s[rows, lanes]
            ni = lr * xi + li * xr + xi_s[rows, lanes]
            if store:
                xr_s[rows, lanes] = nr
                xi_s[rows, lanes] = ni
            return nr, ni

        return lax.fori_loop(0, steps, step, (init_r, init_i), unroll=4)

    if chained:
        @pl.when(pl.program_id(1) == 0)
        def _():
            carr_s[...] = x0r_ref[...]
            cari_s[...] = x0i_ref[...]

    zeros = jnp.zeros((SUBLANES, S5_LANE_GROUP), F32)
    for lo in range(0, S5_FLAT, S5_LANE_GROUP):
        lanes = slice(lo, lo + S5_LANE_GROUP)
        if chained:
            end_r, end_i = scan(zeros, zeros, lo, store=False)
            c_r = carr_s[:, lanes]
            c_i = cari_s[:, lanes]
            ljr = ljr_ref[:, lanes]
            lji = lji_ref[:, lanes]
            for s in range(SUBLANES):
                cr_s[s:s + 1, lanes] = c_r
                ci_s[s:s + 1, lanes] = c_i
                n_r = ljr * c_r - lji * c_i + end_r[s:s + 1, :]
                n_i = ljr * c_i + lji * c_r + end_i[s:s + 1, :]
                c_r, c_i = n_r, n_i
            carr_s[:, lanes] = c_r
            cari_s[:, lanes] = c_i
            scan(cr_s[:, lanes], ci_s[:, lanes], lo, store=True)
        else:
            f_r, f_i = scan(x0r_ref[:, lanes], x0i_ref[:, lanes], lo, store=True)
            xlr_ref[:, lanes] = f_r
            xli_ref[:, lanes] = f_i
    if chained:
        xlr_ref[...] = carr_s[...]
        xli_ref[...] = cari_s[...]

    ys = []
    for j in range(n_bundles):
        sl = slice(j * st_w, (j + 1) * st_w)
        ys.append(_mm(xr_s[:, sl], wcr_ref[j]) + _mm(xi_s[:, sl], wci_ref[j]))
    y = jnp.concatenate(ys, axis=1) + d_ref[...] * u
    y = _gelu_tanh(y)
    y_ref[...] = y * jax.nn.sigmoid(_mm(y, wg_ref[...]))


def _s5_mixer(u, x0r, x0i, prm, *, steps, chained):
    n_groups, rows, _ = u.shape
    t = SUBLANES * steps
    n_chunks = rows // t

    def swap_row_order(a, inner, outer):
        return a.reshape(n_groups, n_chunks, inner, outer, -1).transpose(0, 1, 3, 2, 4).reshape(a.shape)

    u = swap_row_order(u, SUBLANES, steps)
    r0 = x0r.shape[1]
    chunk = pl.BlockSpec((None, t, S5_WIDTH), lambda b, c: (b, c, 0))
    state = pl.BlockSpec((None, r0, S5_FLAT), lambda b, c: (b, 0, 0))
    n_bundles = S5_GROUPS // S5_BUNDLE
    in_w = S5_BUNDLE * S5_GROUP
    st_w = S5_BUNDLE * S5_STATE
    body = functools.partial(_s5_body, steps=steps, chained=chained)
    y, xl_r, xl_i = pl.pallas_call(
        body,
        out_shape=[jax.ShapeDtypeStruct(u.shape, F32),
                   jax.ShapeDtypeStruct(x0r.shape, F32), jax.ShapeDtypeStruct(x0r.shape, F32)],
        grid=(n_groups, n_chunks),
        in_specs=[chunk, state, state,
                  _const_spec((SUBLANES, S5_FLAT)), _const_spec((SUBLANES, S5_FLAT)),
                  _const_spec((1, S5_FLAT)), _const_spec((1, S5_FLAT)),
                  _const_spec((n_bundles, in_w, st_w)), _const_spec((n_bundles, in_w, st_w)),
                  _const_spec((n_bundles, st_w, in_w)), _const_spec((n_bundles, st_w, in_w)),
                  _const_spec((1, S5_WIDTH)), _const_spec((S5_WIDTH, S5_WIDTH))],
        out_specs=[chunk, state, state],
        scratch_shapes=[pltpu.VMEM((t, S5_FLAT), F32), pltpu.VMEM((t, S5_FLAT), F32),
                        pltpu.VMEM((SUBLANES, S5_FLAT), F32), pltpu.VMEM((SUBLANES, S5_FLAT), F32),
                        pltpu.VMEM((1, S5_FLAT), F32), pltpu.VMEM((1, S5_FLAT), F32)],
        compiler_params=_params(2),
        name="s5_chained" if chained else "s5_independent",
    )(u, x0r, x0i, prm["lam_r8"], prm["lam_i8"], prm["lj_r"][steps], prm["lj_i"][steps],
      prm["wb_r"], prm["wb_i"], prm["wc_r"], prm["wc_i"], prm["d"], prm["w_glu"])
    return swap_row_order(y, steps, SUBLANES), xl_r, xl_i


def _s5_prepare(lam_re, lam_im, log_dt, b_re, b_im, c_re, c_im, d_skip, w_glu, step_counts):
    a = jnp.minimum(lam_re.astype(F32), -1e-4)
    b = lam_im.astype(F32)
    dt = jnp.exp(log_dt.astype(F32))[:, None]

    def lam_bar_power(n):
        mag = jnp.exp(a * dt * n)
        return mag * jnp.cos(b * dt * n), mag * jnp.sin(b * dt * n)

    e_r, e_i = lam_bar_power(1.0)
    den = a * a + b * b
    cf_r = ((e_r - 1.0) * a + e_i * b) / den
    cf_i = (e_i * a - (e_r - 1.0) * b) / den
    bb_r = cf_r[..., None] * b_re.astype(F32) - cf_i[..., None] * b_im.astype(F32)
    bb_i = cf_r[..., None] * b_im.astype(F32) + cf_i[..., None] * b_re.astype(F32)
    n_bundles = S5_GROUPS // S5_BUNDLE
    eye = jnp.eye(S5_BUNDLE, dtype=F32)

    def expand_b(t):
        t = t.reshape(n_bundles, S5_BUNDLE, S5_STATE, S5_GROUP)
        w = jnp.einsum("jgpc,gh->jgchp", t, eye)
        return w.reshape(n_bundles, S5_BUNDLE * S5_GROUP, S5_BUNDLE * S5_STATE).astype(BF16)

    def expand_c(t):
        t = t.reshape(n_bundles, S5_BUNDLE, S5_GROUP, S5_STATE)
        w = jnp.einsum("jgcp,gh->jgphc", t, eye)
        return w.reshape(n_bundles, S5_BUNDLE * S5_STATE, S5_BUNDLE * S5_GROUP).astype(BF16)

    flat = lambda t: t.reshape(1, S5_FLAT)
    lj_r, lj_i = {}, {}
    for steps in step_counts:
        p_r, p_i = lam_bar_power(float(steps))
        lj_r[steps] = flat(p_r)
        lj_i[steps] = flat(p_i)
    return {
        "lam_r8": jnp.broadcast_to(flat(e_r), (SUBLANES, S5_FLAT)),
        "lam_i8": jnp.broadcast_to(flat(e_i), (SUBLANES, S5_FLAT)),
        "lj_r": lj_r, "lj_i": lj_i,
        "wb_r": expand_b(bb_r), "wb_i": expand_b(bb_i),
        "wc_r": expand_c(c_re.astype(F32)), "wc_i": expand_c(-c_im.astype(F32)),
        "d": d_skip.astype(F32).reshape(1, S5_WIDTH),
        "w_glu": w_glu.astype(BF16),
    }


def _retention_body(q_ref, k_ref, v_ref, g_ref, s0_ref, dm_ref, qd_ref, kd_ref, cd_ref,
                    y_ref, sout_ref, s_s):
    @pl.when(pl.program_id(1) == 0)
    def _():
        s_s[...] = s0_ref[...]

    for h in range(RET_HEADS):
        qk = slice(h * RET_DK, (h + 1) * RET_DK)
        vv = slice(h * RET_DV, (h + 1) * RET_DV)
        q = q_ref[:, qk].astype(BF16)
        k = k_ref[:, qk]
        v = v_ref[:, vv].astype(BF16)
        s = s_s[h]
        inner = lax.dot_general(q, k.astype(BF16), NT_DIMS, preferred_element_type=F32) * dm_ref[h]
        o = (jnp.dot(inner.astype(BF16), v, preferred_element_type=F32)
             + jnp.dot(q, s.astype(BF16), preferred_element_type=F32) * qd_ref[h])
        k_dec = (k * kd_ref[h]).astype(BF16)
        s_s[h] = s * cd_ref[h] + lax.dot_general(k_dec, v, TN_DIMS, preferred_element_type=F32)
        mu = jnp.mean(o, axis=-1, keepdims=True)
        var = jnp.mean(jnp.square(o - mu), axis=-1, keepdims=True)
        y_ref[:, vv] = g_ref[:, vv] * ((o - mu) * lax.rsqrt(var + NORM_EPS))
    sout_ref[...] = s_s[...]


def _retention(q, k, v, g, s0, chunk):
    bsz, seq_len, _ = q.shape
    n_chunks = seq_len // chunk
    log_g = jnp.log(1.0 - 2.0 ** (-5.0 - jnp.arange(RET_HEADS, dtype=F32)))
    idx = jnp.arange(chunk, dtype=F32)
    diff = idx[:, None] - idx[None, :]
    decay_mask = jnp.where(diff >= 0, jnp.exp(log_g[:, None, None] * jnp.maximum(diff, 0.0)), 0.0)
    q_decay = jnp.exp(log_g[:, None] * (idx + 1.0))[:, :, None]
    k_decay = jnp.exp(log_g[:, None] * (chunk - 1.0 - idx))[:, :, None]
    chunk_decay = jnp.broadcast_to(jnp.exp(log_g * chunk)[:, None, None], (RET_HEADS, 1, RET_DV))
    rows = lambda width: pl.BlockSpec((None, chunk, width), lambda b, c: (b, c, 0))
    state = pl.BlockSpec((None, RET_HEADS, RET_DK, RET_DV), lambda b, c: (b, 0, 0, 0))
    return pl.pallas_call(
        _retention_body,
        out_shape=[jax.ShapeDtypeStruct(v.shape, F32), jax.ShapeDtypeStruct(s0.shape, F32)],
        grid=(bsz, n_chunks),
        in_specs=[rows(RET_QK_WIDTH), rows(RET_QK_WIDTH), rows(RET_V_WIDTH), rows(RET_V_WIDTH), state,
                  _const_spec((RET_HEADS, chunk, chunk)), _const_spec((RET_HEADS, chunk, 1)),
                  _const_spec((RET_HEADS, chunk, 1)), _const_spec((RET_HEADS, 1, RET_DV))],
        out_specs=[rows(RET_V_WIDTH), state],
        scratch_shapes=[pltpu.VMEM((RET_HEADS, RET_DK, RET_DV), F32)],
        compiler_params=_params(2),
        name=f"retention_c{chunk}",
    )(q, k, v, g, s0.astype(F32), decay_mask, q_decay, k_decay, chunk_decay)


def _topk_rank(gate, axis, n):
    idx = lax.broadcasted_iota(jnp.int32, gate.shape, axis)
    rank = jnp.zeros(gate.shape, jnp.int32)
    for other in range(n):
        g_o = gate[:, other:other + 1] if axis == 1 else gate[other:other + 1, :]
        beats = (g_o > gate) | ((g_o == gate) & (other < idx))
        rank = rank + beats.astype(jnp.int32)
    return rank


def _moba_prompt_body(q_ref, k_ref, vt_ref, ksum_ref, o_ref, w_s, sel_s, s_s, p_s, acc_s, *, n_blocks):
    qb = pl.program_id(1)
    own = (qb * PROMPT_Q_TILE) // MOBA_BLOCK
    q_off = (qb * PROMPT_Q_TILE) % MOBA_BLOCK
    key_i = lax.broadcasted_iota(jnp.int32, (MOBA_BLOCK, PROMPT_Q_TILE), 0)
    q_i = lax.broadcasted_iota(jnp.int32, (MOBA_BLOCK, PROMPT_Q_TILE), 1)
    causal = key_i <= q_off + q_i
    blk = lax.broadcasted_iota(jnp.int32, (n_blocks, PROMPT_Q_TILE), 0)
    scale = MOBA_HEAD_DIM ** -0.5
    q_t = q_ref[...].T
    zero_half = jnp.zeros((MOBA_HEAD_DIM, PROMPT_Q_TILE), F32)
    for h in range(MOBA_HEADS):
        hd = slice(h * MOBA_HEAD_DIM, (h + 1) * MOBA_HEAD_DIM)
        q_h = q_t[hd, :]
        kmean = ksum_ref[:, hd] * (1.0 / MOBA_BLOCK)
        gate = jnp.dot(kmean, q_h, precision=lax.Precision.HIGHEST, preferred_element_type=F32)
        gate = jnp.where(blk < own, gate, NEG_INF)
        sel_s[h] = ((_topk_rank(gate, 0, n_blocks) < MOBA_TOPK) & (blk < own)).astype(F32)
        halves = [q_h * scale, zero_half] if h % 2 == 0 else [zero_half, q_h * scale]
        w_s[h] = jnp.concatenate(halves, axis=0).astype(BF16)
    heads = range(MOBA_HEADS)
    hd_of = lambda h: slice(h * MOBA_HEAD_DIM, (h + 1) * MOBA_HEAD_DIM)

    def scores(slot, n):
        rows = pl.ds(pl.multiple_of(n * MOBA_BLOCK, MOBA_BLOCK), MOBA_BLOCK)
        for h in heads:
            pair = slice((h // 2) * LANES, (h // 2 + 1) * LANES)
            s_s[slot, h] = jnp.dot(k_ref[rows, pair], w_s[h], preferred_element_type=F32)

    scores(0, own)
    m, l = [], []
    for h in heads:
        s = jnp.where(causal, s_s[0, h], NEG_INF)
        m_h = jnp.max(s, axis=0, keepdims=True)
        p = jnp.exp(s - m_h)
        p_s[h, 0:MOBA_BLOCK, :] = p.astype(BF16)
        m.append(m_h)
        l.append(jnp.sum(p, axis=0, keepdims=True))
    for h in heads:
        acc_s[hd_of(h), :] = jnp.dot(vt_ref[own, hd_of(h), :], p_s[h, 0:MOBA_BLOCK, :],
                                     preferred_element_type=F32)

    def past_pair(t, carry):
        m, l = carry
        blocks = (2 * t, jnp.minimum(2 * t + 1, n_blocks - 1))
        valid = (True, 2 * t + 1 < own)
        for slot, n in enumerate(blocks):
            scores(slot, n)
        m_out, l_out, alphas = [], [], []
        for h in heads:
            chosen = [(sel_s[h, pl.ds(n, 1), :] > 0.5) & ok for n, ok in zip(blocks, valid)]
            tops = [jnp.where(c, jnp.max(s_s[slot, h], axis=0, keepdims=True), NEG_INF)
                    for slot, c in enumerate(chosen)]
            m_new = jnp.maximum(m[h], jnp.maximum(tops[0], tops[1]))
            alpha = jnp.exp(m[h] - m_new)
            l_new = alpha * l[h]
            for slot, c in enumerate(chosen):
                p = jnp.exp(s_s[slot, h] - jnp.where(c, m_new, -NEG_INF))
                p_s[h, slot * MOBA_BLOCK:(slot + 1) * MOBA_BLOCK, :] = p.astype(BF16)
                l_new = l_new + jnp.sum(p, axis=0, keepdims=True)
            m_out.append(m_new)
            l_out.append(l_new)
            alphas.append(alpha)
        for h in heads:
            vt = jnp.concatenate([vt_ref[n, hd_of(h), :] for n in blocks], axis=1)
            acc_s[hd_of(h), :] = alphas[h] * acc_s[hd_of(h), :] + jnp.dot(vt, p_s[h],
                                                                        preferred_element_type=F32)
        return tuple(m_out), tuple(l_out)

    m, l = lax.fori_loop(0, (own + 1) // 2, past_pair, (tuple(m), tuple(l)))
    o_ref[...] = jnp.concatenate([acc_s[hd_of(h), :] / l[h] for h in heads], axis=0).T


def _moba_prompt(q, kb, vt, ksum):
    bsz, seq_len, _ = q.shape
    n_blocks = seq_len // MOBA_BLOCK
    qblock = pl.BlockSpec((None, PROMPT_Q_TILE, MOBA_WIDTH), lambda b, i: (b, i, 0))
    return pl.pallas_call(
        functools.partial(_moba_prompt_body, n_blocks=n_blocks),
        out_shape=jax.ShapeDtypeStruct(q.shape, F32),
        grid=(bsz, seq_len // PROMPT_Q_TILE),
        in_specs=[qblock,
                  pl.BlockSpec((None, seq_len, MOBA_WIDTH), lambda b, i: (b, 0, 0)),
                  pl.BlockSpec((None, n_blocks, MOBA_WIDTH, MOBA_BLOCK), lambda b, i: (b, 0, 0, 0)),
                  pl.BlockSpec((None, n_blocks, MOBA_WIDTH), lambda b, i: (b, 0, 0))],
        out_specs=qblock,
        scratch_shapes=[pltpu.VMEM((MOBA_HEADS, LANES, PROMPT_Q_TILE), BF16),
                        pltpu.VMEM((MOBA_HEADS, n_blocks, PROMPT_Q_TILE), F32),
                        pltpu.VMEM((2, MOBA_HEADS, MOBA_BLOCK, PROMPT_Q_TILE), F32),
                        pltpu.VMEM((MOBA_HEADS, 2 * MOBA_BLOCK, PROMPT_Q_TILE), BF16),
                        pltpu.VMEM((MOBA_WIDTH, PROMPT_Q_TILE), F32)],
        compiler_params=_params(2),
        name="moba_prompt",
    )(q, kb, vt, ksum)


def _moba_sample_body(pt_ref, q_ref, kn_ref, vn_ref, *rest, n_pages, dec_seq):
    del pt_ref
    pages_per_block = MOBA_BLOCK // PAGE_SIZE
    pps = SAMPLE_BLOCKS_PER_STEP * pages_per_block
    ck_refs = rest[:pps]
    cv_refs = rest[pps:2 * pps]
    o_ref, qbd_s, qs_s, kmt_s, m_s, l_s, acc_s, s_s, p_s = rest[2 * pps:]
    step = pl.program_id(1)
    n_steps = n_pages // pps
    n_blocks = n_pages // pages_per_block
    n_cols = MOBA_HEADS * dec_seq
    scale = MOBA_HEAD_DIM ** -0.5
    r_i = lax.broadcasted_iota(jnp.int32, (n_cols, MOBA_WIDTH), 0)
    c_i = lax.broadcasted_iota(jnp.int32, (n_cols, MOBA_WIDTH), 1)
    head_diag = (r_i // dec_seq) == (c_i // MOBA_HEAD_DIM)

    @pl.when(step == 0)
    def _():
        q_rep = jnp.concatenate([q_ref[...]] * MOBA_HEADS, axis=0)
        q_bd = jnp.where(head_diag, q_rep, 0.0)
        qbd_s[...] = q_bd
        qs_s[...] = (q_bd * scale).astype(BF16)
        m_s[...] = jnp.full(m_s.shape, NEG_INF, F32)
        l_s[...] = jnp.zeros(l_s.shape, F32)

    qs = qs_s[...]
    blk_lane = lax.broadcasted_iota(jnp.int32, (n_cols, n_blocks), 1)
    kmt_lane = lax.broadcasted_iota(jnp.int32, (MOBA_WIDTH, n_blocks), 1)
    blocks = range(SAMPLE_BLOCKS_PER_STEP)
    page_of = lambda refs, j: jnp.concatenate(
        [refs[j * pages_per_block + i][...].reshape(MOBA_WIDTH, PAGE_SIZE) for i in range(pages_per_block)],
        axis=1)
    kmeans, tops, sums = [], [], []
    for j in blocks:
        kt = page_of(ck_refs, j)
        kmeans.append(jnp.sum(kt, axis=1, keepdims=True) * (1.0 / MOBA_BLOCK))
        s_s[j] = jnp.dot(qs, kt.astype(BF16), preferred_element_type=F32)
    for j in blocks:
        s = s_s[j]
        m_n = jnp.max(s, axis=1, keepdims=True)
        p = jnp.exp(s - m_n)
        p_s[j] = p.astype(BF16)
        tops.append(m_n)
        sums.append(jnp.sum(p, axis=1, keepdims=True))
    for j in blocks:
        acc_s[step * SAMPLE_BLOCKS_PER_STEP + j] = lax.dot_general(
            p_s[j], page_of(cv_refs, j).astype(BF16), NT_DIMS, preferred_element_type=F32)
    kmt, m_all, l_all = kmt_s[...], m_s[...], l_s[...]
    for j in range(SAMPLE_BLOCKS_PER_STEP):
        block = step * SAMPLE_BLOCKS_PER_STEP + j
        kmt = jnp.where(kmt_lane == block, kmeans[j], kmt)
        m_all = jnp.where(blk_lane == block, tops[j], m_all)
        l_all = jnp.where(blk_lane == block, sums[j], l_all)
    kmt_s[...] = kmt
    m_s[...] = m_all
    l_s[...] = l_all

    @pl.when(step == n_steps - 1)
    def _():
        gate = jnp.dot(qbd_s[...], kmt_s[...], precision=lax.Precision.HIGHEST,
                       preferred_element_type=F32)
        sel = _topk_rank(gate, 1, n_blocks) < MOBA_TOPK
        s_own = lax.dot_general(qs_s[...], kn_ref[...].astype(BF16), NT_DIMS, preferred_element_type=F32)
        o_row = lax.broadcasted_iota(jnp.int32, s_own.shape, 0)
        o_lane = lax.broadcasted_iota(jnp.int32, s_own.shape, 1)
        s_own = jnp.where(o_lane <= (o_row % dec_seq), s_own, NEG_INF)
        m_own = jnp.max(s_own, axis=1, keepdims=True)
        p_own = jnp.exp(s_own - m_own)
        l_own = jnp.sum(p_own, axis=1, keepdims=True)
        acc_own = jnp.dot(p_own.astype(BF16), vn_ref[...].astype(BF16), preferred_element_type=F32)
        m_all = m_s[...]
        top = jnp.maximum(m_own, jnp.max(jnp.where(sel, m_all, NEG_INF), axis=1, keepdims=True))
        w = jnp.exp(jnp.where(sel, m_all - top, NEG_INF))
        w_own = jnp.exp(m_own - top)
        denom = w_own * l_own + jnp.sum(w * l_s[...], axis=1, keepdims=True)
        out = w_own * acc_own
        for n in range(n_blocks):
            out = out + w[:, n:n + 1] * acc_s[n]
        out = jnp.where(head_diag, out / denom, 0.0)
        res = out[0:dec_seq]
        for h in range(1, MOBA_HEADS):
            res = res + out[h * dec_seq:(h + 1) * dec_seq]
        o_ref[...] = res


def _moba_sample(layer, page_table, q, k_new, v_new, cache_kt, cache_vt):
    bsz, dec_seq, _ = q.shape
    n_pages = page_table.shape[1]
    pages_per_block = MOBA_BLOCK // PAGE_SIZE
    pps = SAMPLE_BLOCKS_PER_STEP * pages_per_block
    n_steps = n_pages // pps
    n_blocks = n_pages // pages_per_block
    n_cols = MOBA_HEADS * dec_seq
    tok = pl.BlockSpec((None, dec_seq, MOBA_WIDTH), lambda b, s, pt: (b, 0, 0))

    def page_spec(i):
        return pl.BlockSpec((None, None, MOBA_HEADS, MOBA_HEAD_DIM, PAGE_SIZE),
                            lambda b, s, pt: (layer, pt[b, s * pps + i], 0, 0, 0))

    grid_spec = pltpu.PrefetchScalarGridSpec(
        num_scalar_prefetch=1,
        grid=(bsz, n_steps),
        in_specs=[tok, tok, tok] + [page_spec(i) for i in range(pps)] * 2,
        out_specs=tok,
        scratch_shapes=[pltpu.VMEM((n_cols, MOBA_WIDTH), F32),
                        pltpu.VMEM((n_cols, MOBA_WIDTH), BF16),
                        pltpu.VMEM((MOBA_WIDTH, n_blocks), F32),
                        pltpu.VMEM((n_cols, n_blocks), F32),
                        pltpu.VMEM((n_cols, n_blocks), F32),
                        pltpu.VMEM((n_blocks, n_cols, MOBA_WIDTH), F32),
                        pltpu.VMEM((SAMPLE_BLOCKS_PER_STEP, n_cols, MOBA_BLOCK), F32),
                        pltpu.VMEM((SAMPLE_BLOCKS_PER_STEP, n_cols, MOBA_BLOCK), BF16)],
    )
    return pl.pallas_call(
        functools.partial(_moba_sample_body, n_pages=n_pages, dec_seq=dec_seq),
        out_shape=jax.ShapeDtypeStruct(q.shape, F32),
        grid_spec=grid_spec,
        compiler_params=_params(2),
        name="moba_sample",
    )(page_table, q, k_new, v_new, *([cache_kt] * pps), *([cache_vt] * pps))


def _merge_body(x_ref, ys_ref, ym_ref, yr_ref, gate_ref, wbs_ref, wbm_ref, wbr_ref, wo_ref, g2_ref,
                wu_ref, wd_ref, gf_ref, o_ref, *, final):
    merged = (gate_ref[:, 0:D_MODEL] * _mm(ys_ref[...], wbs_ref[...])
              + gate_ref[:, D_MODEL:2 * D_MODEL] * _mm(ym_ref[...], wbm_ref[...])
              + gate_ref[:, 2 * D_MODEL:3 * D_MODEL] * _mm(yr_ref[...], wbr_ref[...]))
    x = x_ref[...] + _mm(merged, wo_ref[...])
    up = _mm(_rmsnorm(x, g2_ref[...]), wu_ref[...])
    x = x + _mm(jnp.square(jnp.maximum(up, 0.0)), wd_ref[...])
    if final:
        x = _rmsnorm(x, gf_ref[...])
    o_ref[...] = x


def _merge(x, y_s5, y_moba, y_ret, gates, w, final):
    n = x.shape[0]
    row = lambda width: pl.BlockSpec((ROW_TILE, width), lambda i: (i, 0))
    return pl.pallas_call(
        functools.partial(_merge_body, final=final),
        out_shape=jax.ShapeDtypeStruct(x.shape, F32),
        grid=(n // ROW_TILE,),
        in_specs=[row(D_MODEL), row(S5_WIDTH), row(MOBA_WIDTH), row(RET_V_WIDTH), row(3 * D_MODEL),
                  _const_spec((S5_WIDTH, D_MODEL)), _const_spec((MOBA_WIDTH, D_MODEL)),
                  _const_spec((RET_V_WIDTH, D_MODEL)), _const_spec((D_MODEL, D_MODEL)),
                  _const_spec((1, D_MODEL)), _const_spec((D_MODEL, D_FF)), _const_spec((D_FF, D_MODEL)),
                  _const_spec((1, D_MODEL))],
        out_specs=row(D_MODEL),
        compiler_params=_params(1),
        name="merge_mlp",
    )(x, y_s5, y_moba, y_ret, gates, w["w_br_s5"], w["w_br_moba"], w["w_br_ret"], w["w_out"],
      w["norm2_g"], w["w_mlp_up"], w["w_mlp_down"], w["final_g"])


def _rope_tables(pos):
    half = MOBA_HEAD_DIM // 2
    freqs = ROPE_THETA ** (-jnp.arange(half, dtype=F32) / half)
    ang = pos.astype(F32)[:, None] * freqs[None, :]
    cos = jnp.cos(ang)
    sin = jnp.sin(ang)
    reps = LANES // MOBA_HEAD_DIM
    return (jnp.concatenate([cos, cos] * reps, axis=1), jnp.concatenate([-sin, sin] * reps, axis=1))


def kernel(x_prompt, x_sample, cache_k, cache_v, state_s5_re, state_s5_im, state_ret, page_table,
           norm1_g, w_in, s5_lambda_re, s5_lambda_im, s5_log_dt, s5_b_re, s5_b_im, s5_c_re, s5_c_im,
           s5_d, s5_w_glu, w_br_s5, w_br_moba, w_br_ret, w_out, norm2_g, w_mlp_up, w_mlp_down, final_g):
    bsz_p, seq_p, _ = x_prompt.shape
    bsz_s, seq_s, _ = x_sample.shape
    depth = w_in.shape[0]
    n_pages = page_table.shape[1]
    past_len = n_pages * PAGE_SIZE
    n_p = bsz_p * seq_p
    n_s = bsz_s * seq_s
    prompt_steps = ROW_TILE // SUBLANES
    assert seq_p % ROW_TILE == 0 and n_s % ROW_TILE == 0 and seq_p % RET_CHUNK == 0
    assert ROW_TILE == MOBA_BLOCK
    assert MOBA_BLOCK % PROMPT_Q_TILE == 0 and seq_p % PROMPT_Q_TILE == 0
    assert n_pages % (SAMPLE_BLOCKS_PER_STEP * MOBA_BLOCK // PAGE_SIZE) == 0
    assert seq_s == SUBLANES and bsz_s % SUBLANES == 0 and seq_s % RET_CHUNK != 0

    cos_p, sin_p = _rope_tables(jnp.tile(jnp.arange(seq_p, dtype=jnp.int32), bsz_p))
    cos_s, sin_s = _rope_tables(jnp.tile(past_len + jnp.arange(seq_s, dtype=jnp.int32), bsz_s))
    zeros_s5 = jnp.zeros((bsz_p, 1, S5_FLAT), F32)
    zeros_ret = jnp.zeros((bsz_p, RET_HEADS, RET_DK, RET_DV), F32)
    cache_kt = cache_k.transpose(0, 1, 3, 4, 2)
    cache_vt = cache_v.transpose(0, 1, 3, 4, 2)

    hp = x_prompt.reshape(n_p, D_MODEL)
    hs = x_sample.reshape(n_s, D_MODEL)
    outs = {name: [] for name in ("kp", "vp", "ks", "vs", "s5rp", "s5ip", "s5rs", "s5is", "retp", "rets")}
    for l in range(depth):
        last = l == depth - 1
        w = {"w_br_s5": w_br_s5[l].astype(BF16), "w_br_moba": w_br_moba[l].astype(BF16),
             "w_br_ret": w_br_ret[l].astype(BF16), "w_out": w_out[l].astype(BF16),
             "norm2_g": norm2_g[l].reshape(1, D_MODEL), "w_mlp_up": w_mlp_up[l].astype(BF16),
             "w_mlp_down": w_mlp_down[l].astype(BF16), "final_g": final_g.reshape(1, D_MODEL)}
        g1 = norm1_g[l].reshape(1, D_MODEL)
        w_in_l = w_in[l].astype(BF16)
        s5p = _s5_prepare(s5_lambda_re[l], s5_lambda_im[l], s5_log_dt[l], s5_b_re[l], s5_b_im[l],
                          s5_c_re[l], s5_c_im[l], s5_d[l], s5_w_glu[l], (prompt_steps, seq_s))

        n_blk = seq_p // MOBA_BLOCK
        (u, qm, km_t, vm_t, kb, vt, qr, kr, vr, gr, gates, ksum) = _inproj(hp, g1, w_in_l, cos_p, sin_p,
                                                                           seq_tiles=n_blk)
        seq = lambda t: t.reshape(bsz_p, seq_p, t.shape[-1])
        y_s5, s5r, s5i = _s5_mixer(seq(u), zeros_s5, zeros_s5, s5p, steps=prompt_steps, chained=True)
        y_moba = _moba_prompt(seq(qm), seq(kb), vt, ksum.reshape(bsz_p, n_blk, MOBA_WIDTH))
        y_ret, ret_s = _retention(seq(qr), seq(kr), seq(vr), seq(gr), zeros_ret, RET_CHUNK)
        hp = _merge(hp, y_s5.reshape(n_p, -1), y_moba.reshape(n_p, -1), y_ret.reshape(n_p, -1), gates, w, last)
        per_head = lambda t: t.reshape(bsz_p, MOBA_HEADS, MOBA_HEAD_DIM, seq_p).transpose(0, 3, 1, 2)
        outs["kp"].append(per_head(km_t))
        outs["vp"].append(per_head(vm_t))
        outs["s5rp"].append(s5r.reshape(bsz_p, S5_GROUPS, S5_STATE))
        outs["s5ip"].append(s5i.reshape(bsz_p, S5_GROUPS, S5_STATE))
        outs["retp"].append(ret_s)

        (u, qm, km, vm, kb, vb, qr, kr, vr, gr, gates, ksum) = _inproj(hs, g1, w_in_l, cos_s, sin_s)
        grp = lambda t: t.reshape(bsz_s // SUBLANES, SUBLANES * seq_s, t.shape[-1])
        st = lambda t: t.astype(F32).reshape(bsz_s // SUBLANES, SUBLANES, S5_FLAT)
        y_s5, s5r, s5i = _s5_mixer(grp(u), st(state_s5_re[l]), st(state_s5_im[l]), s5p,
                                   steps=seq_s, chained=False)
        seq = lambda t: t.reshape(bsz_s, seq_s, t.shape[-1])
        y_moba = _moba_sample(l, page_table, seq(qm), seq(km), seq(vm), cache_kt, cache_vt)
        y_ret, ret_s = _retention(seq(qr), seq(kr), seq(vr), seq(gr), state_ret[l], seq_s)
        hs = _merge(hs, y_s5.reshape(n_s, -1), y_moba.reshape(n_s, -1), y_ret.reshape(n_s, -1), gates, w, last)
        outs["ks"].append(km.reshape(bsz_s, seq_s, MOBA_HEADS, MOBA_HEAD_DIM))
        outs["vs"].append(vm.reshape(bsz_s, seq_s, MOBA_HEADS, MOBA_HEAD_DIM))
        outs["s5rs"].append(s5r.reshape(bsz_s, S5_GROUPS, S5_STATE))
        outs["s5is"].append(s5i.reshape(bsz_s, S5_GROUPS, S5_STATE))
        outs["rets"].append(ret_s)

    stack = lambda name: jnp.stack(outs[name])
    return (hp.reshape(x_prompt.shape), hs.reshape(x_sample.shape),
            stack("kp"), stack("vp"), stack("ks"), stack("vs"),
            stack("s5rp"), stack("s5ip"), stack("s5rs"), stack("s5is"), stack("retp"), stack("rets"))
```

```python
import functools
import math

import jax
import jax.numpy as jnp
from jax import lax
from jax.experimental import pallas as pl
from jax.experimental.pallas import tpu as pltpu

F32 = jnp.float32
BF16 = jnp.bfloat16

D_MODEL = 1024
PAGE_SIZE = 128
S5_WIDTH = 512
S5_GROUP = 16
S5_GROUPS = 32
S5_STATE = 64
S5_FLAT = S5_GROUPS * S5_STATE
MOBA_HEADS = 8
MOBA_HEAD_DIM = 64
MOBA_WIDTH = 512
MOBA_BLOCK = 256
MOBA_TOPK = 3
RET_HEADS = 4
RET_DK = 64
RET_DV = 128
RET_QK_WIDTH = 256
RET_V_WIDTH = 512
RET_CHUNK = 128
D_FF = 4096
ROPE_THETA = 10000.0
NORM_EPS = 1e-6
NEG_INF = -1e30
IN_SPLITS = (S5_WIDTH, MOBA_WIDTH, MOBA_WIDTH, MOBA_WIDTH, RET_QK_WIDTH, RET_QK_WIDTH,
             RET_V_WIDTH, RET_V_WIDTH, D_MODEL, D_MODEL, D_MODEL)
IN_COLS = sum(IN_SPLITS)
(OFF_U, OFF_QM, OFF_KM, OFF_VM, OFF_QR, OFF_KR, OFF_VR, OFF_GR, OFF_GATES) = (
    0, 512, 1024, 1536, 2048, 2304, 2560, 3072, 3584)

LANES = 128
SUBLANES = 8
ROW_TILE = 256
S5_BUNDLE = 8
S5_LANE_GROUP = 512
SAMPLE_BLOCKS_PER_STEP = 16
PROMPT_Q_TILE = 256
VMEM_LIMIT = 56 * 1024 * 1024

NT_DIMS = (((1,), (1,)), ((), ()))
TN_DIMS = (((0,), (0,)), ((), ()))


def _const_spec(shape):
    nd = len(shape)
    return pl.BlockSpec(shape, lambda *_: (0,) * nd, pipeline_mode=pl.Buffered(1))


def _params(n_axes):
    return pltpu.CompilerParams(dimension_semantics=("arbitrary",) * n_axes,
                                vmem_limit_bytes=VMEM_LIMIT)


def _rmsnorm(x, g):
    return x * lax.rsqrt(jnp.mean(x * x, axis=-1, keepdims=True) + NORM_EPS) * g


def _mm(a, w):
    return jnp.dot(a.astype(BF16), w, preferred_element_type=F32)


def _rope(t, cos, sin_signed):
    n = t.shape[1]
    reps = n // LANES
    c = jnp.concatenate([cos] * reps, axis=1)
    s = jnp.concatenate([sin_signed] * reps, axis=1)
    lane = lax.broadcasted_iota(jnp.int32, t.shape, 1)
    first_half = (lane % MOBA_HEAD_DIM) < (MOBA_HEAD_DIM // 2)
    partner = jnp.where(first_half, pltpu.roll(t, n - MOBA_HEAD_DIM // 2, 1),
                        pltpu.roll(t, MOBA_HEAD_DIM // 2, 1))
    return t * c + partner * s


def _inproj_body(x_ref, g_ref, w_ref, cos_ref, sin_ref,
                 u_ref, qm_ref, km_ref, vm_ref, kb_ref, vb_ref, qr_ref, kr_ref, vr_ref, gr_ref,
                 gate_ref, ksum_ref, *, transposed_kv):
    hb = _rmsnorm(x_ref[...], g_ref[...]).astype(BF16)
    cos = cos_ref[...]
    sin = sin_ref[...]

    def proj(lo, n):
        return jnp.dot(hb, w_ref[:, lo:lo + n], preferred_element_type=F32)

    u_ref[...] = proj(OFF_U, S5_WIDTH)
    qm_ref[...] = _rope(proj(OFF_QM, MOBA_WIDTH), cos, sin)
    km = _rope(proj(OFF_KM, MOBA_WIDTH), cos, sin)
    kb_ref[...] = km.astype(BF16)
    ksum_ref[0] = jnp.sum(km, axis=0, keepdims=True)
    vm = proj(OFF_VM, MOBA_WIDTH)
    if transposed_kv:
        km, vm = km.T, vm.T
    km_ref[...] = km
    vm_ref[...] = vm
    vb_ref[...] = vm.astype(BF16)
    qr_ref[...] = _rope(proj(OFF_QR, RET_QK_WIDTH), cos, sin)
    kr_ref[...] = _rope(proj(OFF_KR, RET_QK_WIDTH), cos, sin) * (RET_DK ** -0.5)
    vr_ref[...] = proj(OFF_VR, RET_V_WIDTH)
    gr = proj(OFF_GR, RET_V_WIDTH)
    gr_ref[...] = gr * jax.nn.sigmoid(gr)
    for i in range(3):
        gate_ref[:, i * D_MODEL:(i + 1) * D_MODEL] = jax.nn.sigmoid(proj(OFF_GATES + i * D_MODEL, D_MODEL))


def _inproj(x, g, w, cos, sin, seq_tiles=None):
    n = x.shape[0]
    nt = n // ROW_TILE
    row = lambda width: pl.BlockSpec((ROW_TILE, width), lambda i: (i, 0))
    widths = (S5_WIDTH, MOBA_WIDTH, MOBA_WIDTH, MOBA_WIDTH, MOBA_WIDTH, MOBA_WIDTH,
              RET_QK_WIDTH, RET_QK_WIDTH, RET_V_WIDTH, RET_V_WIDTH, 3 * D_MODEL)
    dtypes = (F32, F32, F32, F32, BF16, BF16, F32, F32, F32, F32, F32)
    out_shape = [jax.ShapeDtypeStruct((n, wd), dt) for wd, dt in zip(widths, dtypes)]
    out_shape.append(jax.ShapeDtypeStruct((nt, 1, MOBA_WIDTH), F32))
    out_specs = [row(wd) for wd in widths]
    out_specs.append(pl.BlockSpec((1, 1, MOBA_WIDTH), lambda i: (i, 0, 0)))
    if seq_tiles is not None:
        n_seq = nt // seq_tiles
        kv_t = pl.BlockSpec((None, MOBA_WIDTH, ROW_TILE), lambda i: (i // seq_tiles, 0, i % seq_tiles))
        kv_t_shape = (n_seq, MOBA_WIDTH, seq_tiles * ROW_TILE)
        out_shape[2] = out_shape[3] = jax.ShapeDtypeStruct(kv_t_shape, F32)
        out_specs[2] = out_specs[3] = kv_t
        out_shape[5] = jax.ShapeDtypeStruct((n_seq, seq_tiles, MOBA_WIDTH, ROW_TILE), BF16)
        out_specs[5] = pl.BlockSpec((None, None, MOBA_WIDTH, ROW_TILE),
                                    lambda i: (i // seq_tiles, i % seq_tiles, 0, 0))
    return pl.pallas_call(
        functools.partial(_inproj_body, transposed_kv=seq_tiles is not None),
        out_shape=out_shape,
        grid=(nt,),
        in_specs=[row(D_MODEL), _const_spec((1, D_MODEL)), _const_spec((D_MODEL, IN_COLS)),
                  row(LANES), row(LANES)],
        out_specs=out_specs,
        compiler_params=_params(1),
        name="inproj",
    )(x, g, w, cos, sin)


def _gelu_tanh(x):
    cdf = 0.5 * (1.0 + jnp.tanh(math.sqrt(2.0 / math.pi) * (x + 0.044715 * (x * x * x))))
    return x * cdf


def _s5_body(u_ref, x0r_ref, x0i_ref, lamr_ref, lami_ref, ljr_ref, lji_ref, wbr_ref, wbi_ref,
             wcr_ref, wci_ref, d_ref, wg_ref,
             y_ref, xlr_ref, xli_ref,
             xr_s, xi_s, cr_s, ci_s, carr_s, cari_s, *, steps, chained):
    n_bundles = S5_GROUPS // S5_BUNDLE
    in_w = S5_BUNDLE * S5_GROUP
    st_w = S5_BUNDLE * S5_STATE
    u = u_ref[...]
    ub = u.astype(BF16)
    for j in range(n_bundles):
        uj = ub[:, j * in_w:(j + 1) * in_w]
        xr_s[:, j * st_w:(j + 1) * st_w] = jnp.dot(uj, wbr_ref[j], preferred_element_type=F32)
        xi_s[:, j * st_w:(j + 1) * st_w] = jnp.dot(uj, wbi_ref[j], preferred_element_type=F32)

    def scan(init_r, init_i, lo, store):
        lanes = slice(lo, lo + S5_LANE_GROUP)
        lr = lamr_ref[:, lanes]
        li = lami_ref[:, lanes]

        def step(t, carry):
            xr, xi = carry
            rows = pl.ds(pl.multiple_of(t * SUBLANES, SUBLANES), SUBLANES)
            nr = lr * xr - li * xi + xr_s[rows, lanes]
            ni = lr * xi + li * xr + xi_s[rows, lanes]
            if store:
                xr_s[rows, lanes] = nr
                xi_s[rows, lanes] = ni
            return nr, ni

        return lax.fori_loop(0, steps, step, (init_r, init_i), unroll=4)

    if chained:
        @pl.when(pl.program_id(1) == 0)
        def _():
            carr_s[...] = x0r_ref[...]
            cari_s[...] = x0i_ref[...]

    zeros = jnp.zeros((SUBLANES, S5_LANE_GROUP), F32)
    for lo in range(0, S5_FLAT, S5_LANE_GROUP):
        lanes = slice(lo, lo + S5_LANE_GROUP)
        if chained:
            end_r, end_i = scan(zeros, zeros, lo, store=False)
            c_r = carr_s[:, lanes]
            c_i = cari_s[:, lanes]
            ljr = ljr_ref[:, lanes]
            lji = lji_ref[:, lanes]
            for s in range(SUBLANES):
                cr_s[s:s + 1, lanes] = c_r
                ci_s[s:s + 1, lanes] = c_i
                n_r = ljr * c_r - lji * c_i + end_r[s:s + 1, :]
                n_i = ljr * c_i + lji * c_r + end_i[s:s + 1, :]
                c_r, c_i = n_r, n_i
            carr_s[:, lanes] = c_r
            cari_s[:, lanes] = c_i
            scan(cr_s[:, lanes], ci_s[:, lanes], lo, store=True)
        else:
            f_r, f_i = scan(x0r_ref[:, lanes], x0i_ref[:, lanes], lo, store=True)
            xlr_ref[:, lanes] = f_r
            xli_ref[:, lanes] = f_i
    if chained:
        xlr_ref[...] = carr_s[...]
        xli_ref[...] = cari_s[...]

    ys = []
    for j in range(n_bundles):
        sl = slice(j * st_w, (j + 1) * st_w)
        ys.append(_mm(xr_s[:, sl], wcr_ref[j]) + _mm(xi_s[:, sl], wci_ref[j]))
    y = jnp.concatenate(ys, axis=1) + d_ref[...] * u
    y = _gelu_tanh(y)
    y_ref[...] = y * jax.nn.sigmoid(_mm(y, wg_ref[...]))


def _s5_mixer(u, x0r, x0i, prm, *, steps, chained):
    n_groups, rows, _ = u.shape
    t = SUBLANES * steps
    n_chunks = rows // t

    def swap_row_order(a, inner, outer):
        return a.reshape(n_groups, n_chunks, inner, outer, -1).transpose(0, 1, 3, 2, 4).reshape(a.shape)

    u = swap_row_order(u, SUBLANES, steps)
    r0 = x0r.shape[1]
    chunk = pl.BlockSpec((None, t, S5_WIDTH), lambda b, c: (b, c, 0))
    state = pl.BlockSpec((None, r0, S5_FLAT), lambda b, c: (b, 0, 0))
    n_bundles = S5_GROUPS // S5_BUNDLE
    in_w = S5_BUNDLE * S5_GROUP
    st_w = S5_BUNDLE * S5_STATE
    body = functools.partial(_s5_body, steps=steps, chained=chained)
    y, xl_r, xl_i = pl.pallas_call(
        body,
        out_shape=[jax.ShapeDtypeStruct(u.shape, F32),
                   jax.ShapeDtypeStruct(x0r.shape, F32), jax.ShapeDtypeStruct(x0r.shape, F32)],
        grid=(n_groups, n_chunks),
        in_specs=[chunk, state, state,
                  _const_spec((SUBLANES, S5_FLAT)), _const_spec((SUBLANES, S5_FLAT)),
                  _const_spec((1, S5_FLAT)), _const_spec((1, S5_FLAT)),
                  _const_spec((n_bundles, in_w, st_w)), _const_spec((n_bundles, in_w, st_w)),
                  _const_spec((n_bundles, st_w, in_w)), _const_spec((n_bundles, st_w, in_w)),
                  _const_spec((1, S5_WIDTH)), _const_spec((S5_WIDTH, S5_WIDTH))],
        out_specs=[chunk, state, state],
        scratch_shapes=[pltpu.VMEM((t, S5_FLAT), F32), pltpu.VMEM((t, S5_FLAT), F32),
                        pltpu.VMEM((SUBLANES, S5_FLAT), F32), pltpu.VMEM((SUBLANES, S5_FLAT), F32),
                        pltpu.VMEM((1, S5_FLAT), F32), pltpu.VMEM((1, S5_FLAT), F32)],
        compiler_params=_params(2),
        name="s5_chained" if chained else "s5_independent",
    )(u, x0r, x0i, prm["lam_r8"], prm["lam_i8"], prm["lj_r"][steps], prm["lj_i"][steps],
      prm["wb_r"], prm["wb_i"], prm["wc_r"], prm["wc_i"], prm["d"], prm["w_glu"])
    return swap_row_order(y, steps, SUBLANES), xl_r, xl_i


def _s5_prepare(lam_re, lam_im, log_dt, b_re, b_im, c_re, c_im, d_skip, w_glu, step_counts):
    a = jnp.minimum(lam_re.astype(F32), -1e-4)
    b = lam_im.astype(F32)
    dt = jnp.exp(log_dt.astype(F32))[:, None]

    def lam_bar_power(n):
        mag = jnp.exp(a * dt * n)
        return mag * jnp.cos(b * dt * n), mag * jnp.sin(b * dt * n)

    e_r, e_i = lam_bar_power(1.0)
    den = a * a + b * b
    cf_r = ((e_r - 1.0) * a + e_i * b) / den
    cf_i = (e_i * a - (e_r - 1.0) * b) / den
    bb_r = cf_r[..., None] * b_re.astype(F32) - cf_i[..., None] * b_im.astype(F32)
    bb_i = cf_r[..., None] * b_im.astype(F32) + cf_i[..., None] * b_re.astype(F32)
    n_bundles = S5_GROUPS // S5_BUNDLE
    eye = jnp.eye(S5_BUNDLE, dtype=F32)

    def expand_b(t):
        t = t.reshape(n_bundles, S5_BUNDLE, S5_STATE, S5_GROUP)
        w = jnp.einsum("jgpc,gh->jgchp", t, eye)
        return w.reshape(n_bundles, S5_BUNDLE * S5_GROUP, S5_BUNDLE * S5_STATE).astype(BF16)

    def expand_c(t):
        t = t.reshape(n_bundles, S5_BUNDLE, S5_GROUP, S5_STATE)
        w = jnp.einsum("jgcp,gh->jgphc", t, eye)
        return w.reshape(n_bundles, S5_BUNDLE * S5_STATE, S5_BUNDLE * S5_GROUP).astype(BF16)

    flat = lambda t: t.reshape(1, S5_FLAT)
    lj_r, lj_i = {}, {}
    for steps in step_counts:
        p_r, p_i = lam_bar_power(float(steps))
        lj_r[steps] = flat(p_r)
        lj_i[steps] = flat(p_i)
    return {
        "lam_r8": jnp.broadcast_to(flat(e_r), (SUBLANES, S5_FLAT)),
        "lam_i8": jnp.broadcast_to(flat(e_i), (SUBLANES, S5_FLAT)),
        "lj_r": lj_r, "lj_i": lj_i,
        "wb_r": expand_b(bb_r), "wb_i": expand_b(bb_i),
        "wc_r": expand_c(c_re.astype(F32)), "wc_i": expand_c(-c_im.astype(F32)),
        "d": d_skip.astype(F32).reshape(1, S5_WIDTH),
        "w_glu": w_glu.astype(BF16),
    }


def _retention_body(q_ref, k_ref, v_ref, g_ref, s0_ref, dm_ref, qd_ref, kd_ref, cd_ref,
                    y_ref, sout_ref, s_s):
    @pl.when(pl.program_id(1) == 0)
    def _():
        s_s[...] = s0_ref[...]

    for h in range(RET_HEADS):
        qk = slice(h * RET_DK, (h + 1) * RET_DK)
        vv = slice(h * RET_DV, (h + 1) * RET_DV)
        q = q_ref[:, qk].astype(BF16)
        k = k_ref[:, qk]
        v = v_ref[:, vv].astype(BF16)
        s = s_s[h]
        inner = lax.dot_general(q, k.astype(BF16), NT_DIMS, preferred_element_type=F32) * dm_ref[h]
        o = (jnp.dot(inner.astype(BF16), v, preferred_element_type=F32)
             + jnp.dot(q, s.astype(BF16), preferred_element_type=F32) * qd_ref[h])
        k_dec = (k * kd_ref[h]).astype(BF16)
        s_s[h] = s * cd_ref[h] + lax.dot_general(k_dec, v, TN_DIMS, preferred_element_type=F32)
        mu = jnp.mean(o, axis=-1, keepdims=True)
        var = jnp.mean(jnp.square(o - mu), axis=-1, keepdims=True)
        y_ref[:, vv] = g_ref[:, vv] * ((o - mu) * lax.rsqrt(var + NORM_EPS))
    sout_ref[...] = s_s[...]


def _retention(q, k, v, g, s0, chunk):
    bsz, seq_len, _ = q.shape
    n_chunks = seq_len // chunk
    log_g = jnp.log(1.0 - 2.0 ** (-5.0 - jnp.arange(RET_HEADS, dtype=F32)))
    idx = jnp.arange(chunk, dtype=F32)
    diff = idx[:, None] - idx[None, :]
    decay_mask = jnp.where(diff >= 0, jnp.exp(log_g[:, None, None] * jnp.maximum(diff, 0.0)), 0.0)
    q_decay = jnp.exp(log_g[:, None] * (idx + 1.0))[:, :, None]
    k_decay = jnp.exp(log_g[:, None] * (chunk - 1.0 - idx))[:, :, None]
    chunk_decay = jnp.broadcast_to(jnp.exp(log_g * chunk)[:, None, None], (RET_HEADS, 1, RET_DV))
    rows = lambda width: pl.BlockSpec((None, chunk, width), lambda b, c: (b, c, 0))
    state = pl.BlockSpec((None, RET_HEADS, RET_DK, RET_DV), lambda b, c: (b, 0, 0, 0))
    return pl.pallas_call(
        _retention_body,
        out_shape=[jax.ShapeDtypeStruct(v.shape, F32), jax.ShapeDtypeStruct(s0.shape, F32)],
        grid=(bsz, n_chunks),
        in_specs=[rows(RET_QK_WIDTH), rows(RET_QK_WIDTH), rows(RET_V_WIDTH), rows(RET_V_WIDTH), state,
                  _const_spec((RET_HEADS, chunk, chunk)), _const_spec((RET_HEADS, chunk, 1)),
                  _const_spec((RET_HEADS, chunk, 1)), _const_spec((RET_HEADS, 1, RET_DV))],
        out_specs=[rows(RET_V_WIDTH), state],
        scratch_shapes=[pltpu.VMEM((RET_HEADS, RET_DK, RET_DV), F32)],
        compiler_params=_params(2),
        name=f"retention_c{chunk}",
    )(q, k, v, g, s0.astype(F32), decay_mask, q_decay, k_decay, chunk_decay)


def _topk_rank(gate, axis, n):
    idx = lax.broadcasted_iota(jnp.int32, gate.shape, axis)
    rank = jnp.zeros(gate.shape, jnp.int32)
    for other in range(n):
        g_o = gate[:, other:other + 1] if axis == 1 else gate[other:other + 1, :]
        beats = (g_o > gate) | ((g_o == gate) & (other < idx))
        rank = rank + beats.astype(jnp.int32)
    return rank


def _moba_prompt_body(q_ref, k_ref, vt_ref, ksum_ref, o_ref, w_s, sel_s, s_s, p_s, acc_s, *, n_blocks):
    qb = pl.program_id(1)
    own = (qb * PROMPT_Q_TILE) // MOBA_BLOCK
    q_off = (qb * PROMPT_Q_TILE) % MOBA_BLOCK
    key_i = lax.broadcasted_iota(jnp.int32, (MOBA_BLOCK, PROMPT_Q_TILE), 0)
    q_i = lax.broadcasted_iota(jnp.int32, (MOBA_BLOCK, PROMPT_Q_TILE), 1)
    causal = key_i <= q_off + q_i
    blk = lax.broadcasted_iota(jnp.int32, (n_blocks, PROMPT_Q_TILE), 0)
    scale = MOBA_HEAD_DIM ** -0.5
    q_t = q_ref[...].T
    zero_half = jnp.zeros((MOBA_HEAD_DIM, PROMPT_Q_TILE), F32)
    for h in range(MOBA_HEADS):
        hd = slice(h * MOBA_HEAD_DIM, (h + 1) * MOBA_HEAD_DIM)
        q_h = q_t[hd, :]
        kmean = ksum_ref[:, hd] * (1.0 / MOBA_BLOCK)
        gate = jnp.dot(kmean, q_h, precision=lax.Precision.HIGHEST, preferred_element_type=F32)
        gate = jnp.where(blk < own, gate, NEG_INF)
        sel_s[h] = ((_topk_rank(gate, 0, n_blocks) < MOBA_TOPK) & (blk < own)).astype(F32)
        halves = [q_h * scale, zero_half] if h % 2 == 0 else [zero_half, q_h * scale]
        w_s[h] = jnp.concatenate(halves, axis=0).astype(BF16)
    heads = range(MOBA_HEADS)
    hd_of = lambda h: slice(h * MOBA_HEAD_DIM, (h + 1) * MOBA_HEAD_DIM)

    def scores(slot, n):
        rows = pl.ds(pl.multiple_of(n * MOBA_BLOCK, MOBA_BLOCK), MOBA_BLOCK)
        for h in heads:
            pair = slice((h // 2) * LANES, (h // 2 + 1) * LANES)
            s_s[slot, h] = jnp.dot(k_ref[rows, pair], w_s[h], preferred_element_type=F32)

    scores(0, own)
    m, l = [], []
    for h in heads:
        s = jnp.where(causal, s_s[0, h], NEG_INF)
        m_h = jnp.max(s, axis=0, keepdims=True)
        p = jnp.exp(s - m_h)
        p_s[h, 0:MOBA_BLOCK, :] = p.astype(BF16)
        m.append(m_h)
        l.append(jnp.sum(p, axis=0, keepdims=True))
    for h in heads:
        acc_s[hd_of(h), :] = jnp.dot(vt_ref[own, hd_of(h), :], p_s[h, 0:MOBA_BLOCK, :],
                                     preferred_element_type=F32)

    def past_pair(t, carry):
        m, l = carry
        blocks = (2 * t, jnp.minimum(2 * t + 1, n_blocks - 1))
        valid = (True, 2 * t + 1 < own)
        for slot, n in enumerate(blocks):
            scores(slot, n)
        m_out, l_out, alphas = [], [], []
        for h in heads:
            chosen = [(sel_s[h, pl.ds(n, 1), :] > 0.5) & ok for n, ok in zip(blocks, valid)]
            tops = [jnp.where(c, jnp.max(s_s[slot, h], axis=0, keepdims=True), NEG_INF)
                    for slot, c in enumerate(chosen)]
            m_new = jnp.maximum(m[h], jnp.maximum(tops[0], tops[1]))
            alpha = jnp.exp(m[h] - m_new)
            l_new = alpha * l[h]
            for slot, c in enumerate(chosen):
                p = jnp.exp(s_s[slot, h] - jnp.where(c, m_new, -NEG_INF))
                p_s[h, slot * MOBA_BLOCK:(slot + 1) * MOBA_BLOCK, :] = p.astype(BF16)
                l_new = l_new + jnp.sum(p, axis=0, keepdims=True)
            m_out.append(m_new)
            l_out.append(l_new)
            alphas.append(alpha)
        for h in heads:
            vt = jnp.concatenate([vt_ref[n, hd_of(h), :] for n in blocks], axis=1)
            acc_s[hd_of(h), :] = alphas[h] * acc_s[hd_of(h), :] + jnp.dot(vt, p_s[h],
                                                                        preferred_element_type=F32)
        return tuple(m_out), tuple(l_out)

    m, l = lax.fori_loop(0, (own + 1) // 2, past_pair, (tuple(m), tuple(l)))
    o_ref[...] = jnp.concatenate([acc_s[hd_of(h), :] / l[h] for h in heads], axis=0).T


def _moba_prompt(q, kb, vt, ksum):
    bsz, seq_len, _ = q.shape
    n_blocks = seq_len // MOBA_BLOCK
    qblock = pl.BlockSpec((None, PROMPT_Q_TILE, MOBA_WIDTH), lambda b, i: (b, i, 0))
    return pl.pallas_call(
        functools.partial(_moba_prompt_body, n_blocks=n_blocks),
        out_shape=jax.ShapeDtypeStruct(q.shape, F32),
        grid=(bsz, seq_len // PROMPT_Q_TILE),
        in_specs=[qblock,
                  pl.BlockSpec((None, seq_len, MOBA_WIDTH), lambda b, i: (b, 0, 0)),
                  pl.BlockSpec((None, n_blocks, MOBA_WIDTH, MOBA_BLOCK), lambda b, i: (b, 0, 0, 0)),
                  pl.BlockSpec((None, n_blocks, MOBA_WIDTH), lambda b, i: (b, 0, 0))],
        out_specs=qblock,
        scratch_shapes=[pltpu.VMEM((MOBA_HEADS, LANES, PROMPT_Q_TILE), BF16),
                        pltpu.VMEM((MOBA_HEADS, n_blocks, PROMPT_Q_TILE), F32),
                        pltpu.VMEM((2, MOBA_HEADS, MOBA_BLOCK, PROMPT_Q_TILE), F32),
                        pltpu.VMEM((MOBA_HEADS, 2 * MOBA_BLOCK, PROMPT_Q_TILE), BF16),
                        pltpu.VMEM((MOBA_WIDTH, PROMPT_Q_TILE), F32)],
        compiler_params=_params(2),
        name="moba_prompt",
    )(q, kb, vt, ksum)


def _moba_sample_body(pt_ref, q_ref, kn_ref, vn_ref, *rest, n_pages, dec_seq):
    del pt_ref
    pages_per_block = MOBA_BLOCK // PAGE_SIZE
    pps = SAMPLE_BLOCKS_PER_STEP * pages_per_block
    ck_refs = rest[:pps]
    cv_refs = rest[pps:2 * pps]
    o_ref, qbd_s, qs_s, kmt_s, m_s, l_s, acc_s, s_s, p_s = rest[2 * pps:]
    step = pl.program_id(1)
    n_steps = n_pages // pps
    n_blocks = n_pages // pages_per_block
    n_cols = MOBA_HEADS * dec_seq
    scale = MOBA_HEAD_DIM ** -0.5
    r_i = lax.broadcasted_iota(jnp.int32, (n_cols, MOBA_WIDTH), 0)
    c_i = lax.broadcasted_iota(jnp.int32, (n_cols, MOBA_WIDTH), 1)
    head_diag = (r_i // dec_seq) == (c_i // MOBA_HEAD_DIM)

    @pl.when(step == 0)
    def _():
        q_rep = jnp.concatenate([q_ref[...]] * MOBA_HEADS, axis=0)
        q_bd = jnp.where(head_diag, q_rep, 0.0)
        qbd_s[...] = q_bd
        qs_s[...] = (q_bd * scale).astype(BF16)
        m_s[...] = jnp.full(m_s.shape, NEG_INF, F32)
        l_s[...] = jnp.zeros(l_s.shape, F32)

    qs = qs_s[...]
    blk_lane = lax.broadcasted_iota(jnp.int32, (n_cols, n_blocks), 1)
    kmt_lane = lax.broadcasted_iota(jnp.int32, (MOBA_WIDTH, n_blocks), 1)
    blocks = range(SAMPLE_BLOCKS_PER_STEP)
    page_of = lambda refs, j: jnp.concatenate(
        [refs[j * pages_per_block + i][...].reshape(MOBA_WIDTH, PAGE_SIZE) for i in range(pages_per_block)],
        axis=1)
    kmeans, tops, sums = [], [], []
    for j in blocks:
        kt = page_of(ck_refs, j)
        kmeans.append(jnp.sum(kt, axis=1, keepdims=True) * (1.0 / MOBA_BLOCK))
        s_s[j] = jnp.dot(qs, kt.astype(BF16), preferred_element_type=F32)
    for j in blocks:
        s = s_s[j]
        m_n = jnp.max(s, axis=1, keepdims=True)
        p = jnp.exp(s - m_n)
        p_s[j] = p.astype(BF16)
        tops.append(m_n)
        sums.append(jnp.sum(p, axis=1, keepdims=True))
    for j in blocks:
        acc_s[step * SAMPLE_BLOCKS_PER_STEP + j] = lax.dot_general(
            p_s[j], page_of(cv_refs, j).astype(BF16), NT_DIMS, preferred_element_type=F32)
    kmt, m_all, l_all = kmt_s[...], m_s[...], l_s[...]
    for j in range(SAMPLE_BLOCKS_PER_STEP):
        block = step * SAMPLE_BLOCKS_PER_STEP + j
        kmt = jnp.where(kmt_lane == block, kmeans[j], kmt)
        m_all = jnp.where(blk_lane == block, tops[j], m_all)
        l_all = jnp.where(blk_lane == block, sums[j], l_all)
    kmt_s[...] = kmt
    m_s[...] = m_all
    l_s[...] = l_all

    @pl.when(step == n_steps - 1)
    def _():
        gate = jnp.dot(qbd_s[...], kmt_s[...], precision=lax.Precision.HIGHEST,
                       preferred_element_type=F32)
        sel = _topk_rank(gate, 1, n_blocks) < MOBA_TOPK
        s_own = lax.dot_general(qs_s[...], kn_ref[...].astype(BF16), NT_DIMS, preferred_element_type=F32)
        o_row = lax.broadcasted_iota(jnp.int32, s_own.shape, 0)
        o_lane = lax.broadcasted_iota(jnp.int32, s_own.shape, 1)
        s_own = jnp.where(o_lane <= (o_row % dec_seq), s_own, NEG_INF)
        m_own = jnp.max(s_own, axis=1, keepdims=True)
        p_own = jnp.exp(s_own - m_own)
        l_own = jnp.sum(p_own, axis=1, keepdims=True)
        acc_own = jnp.dot(p_own.astype(BF16), vn_ref[...].astype(BF16), preferred_element_type=F32)
        m_all = m_s[...]
        top = jnp.maximum(m_own, jnp.max(jnp.where(sel, m_all, NEG_INF), axis=1, keepdims=True))
        w = jnp.exp(jnp.where(sel, m_all - top, NEG_INF))
        w_own = jnp.exp(m_own - top)
        denom = w_own * l_own + jnp.sum(w * l_s[...], axis=1, keepdims=True)
        out = w_own * acc_own
        for n in range(n_blocks):
            out = out + w[:, n:n + 1] * acc_s[n]
        out = jnp.where(head_diag, out / denom, 0.0)
        res = out[0:dec_seq]
        for h in range(1, MOBA_HEADS):
            res = res + out[h * dec_seq:(h + 1) * dec_seq]
        o_ref[...] = res


def _moba_sample(layer, page_table, q, k_new, v_new, cache_kt, cache_vt):
    bsz, dec_seq, _ = q.shape
    n_pages = page_table.shape[1]
    pages_per_block = MOBA_BLOCK // PAGE_SIZE
    pps = SAMPLE_BLOCKS_PER_STEP * pages_per_block
    n_steps = n_pages // pps
    n_blocks = n_pages // pages_per_block
    n_cols = MOBA_HEADS * dec_seq
    tok = pl.BlockSpec((None, dec_seq, MOBA_WIDTH), lambda b, s, pt: (b, 0, 0))

    def page_spec(i):
        return pl.BlockSpec((None, None, MOBA_HEADS, MOBA_HEAD_DIM, PAGE_SIZE),
                            lambda b, s, pt: (layer, pt[b, s * pps + i], 0, 0, 0))

    grid_spec = pltpu.PrefetchScalarGridSpec(
        num_scalar_prefetch=1,
        grid=(bsz, n_steps),
        in_specs=[tok, tok, tok] + [page_spec(i) for i in range(pps)] * 2,
        out_specs=tok,
        scratch_shapes=[pltpu.VMEM((n_cols, MOBA_WIDTH), F32),
                        pltpu.VMEM((n_cols, MOBA_WIDTH), BF16),
                        pltpu.VMEM((MOBA_WIDTH, n_blocks), F32),
                        pltpu.VMEM((n_cols, n_blocks), F32),
                        pltpu.VMEM((n_cols, n_blocks), F32),
                        pltpu.VMEM((n_blocks, n_cols, MOBA_WIDTH), F32),
                        pltpu.VMEM((SAMPLE_BLOCKS_PER_STEP, n_cols, MOBA_BLOCK), F32),
                        pltpu.VMEM((SAMPLE_BLOCKS_PER_STEP, n_cols, MOBA_BLOCK), BF16)],
    )
    return pl.pallas_call(
        functools.partial(_moba_sample_body, n_pages=n_pages, dec_seq=dec_seq),
        out_shape=jax.ShapeDtypeStruct(q.shape, F32),
        grid_spec=grid_spec,
        compiler_params=_params(2),
        name="moba_sample",
    )(page_table, q, k_new, v_new, *([cache_kt] * pps), *([cache_vt] * pps))


def _merge_body(x_ref, ys_ref, ym_ref, yr_ref, gate_ref, wbs_ref, wbm_ref, wbr_ref, wo_ref, g2_ref,
                wu_ref, wd_ref, gf_ref, o_ref, *, final):
    merged = (gate_ref[:, 0:D_MODEL] * _mm(ys_ref[...], wbs_ref[...])
              + gate_ref[:, D_MODEL:2 * D_MODEL] * _mm(ym_ref[...], wbm_ref[...])
              + gate_ref[:, 2 * D_MODEL:3 * D_MODEL] * _mm(yr_ref[...], wbr_ref[...]))
    x = x_ref[...] + _mm(merged, wo_ref[...])
    up = _mm(_rmsnorm(x, g2_ref[...]), wu_ref[...])
    x = x + _mm(jnp.square(jnp.maximum(up, 0.0)), wd_ref[...])
    if final:
        x = _rmsnorm(x, gf_ref[...])
    o_ref[...] = x


def _merge(x, y_s5, y_moba, y_ret, gates, w, final):
    n = x.shape[0]
    row = lambda width: pl.BlockSpec((ROW_TILE, width), lambda i: (i, 0))
    return pl.pallas_call(
        functools.partial(_merge_body, final=final),
        out_shape=jax.ShapeDtypeStruct(x.shape, F32),
        grid=(n // ROW_TILE,),
        in_specs=[row(D_MODEL), row(S5_WIDTH), row(MOBA_WIDTH), row(RET_V_WIDTH), row(3 * D_MODEL),
                  _const_spec((S5_WIDTH, D_MODEL)), _const_spec((MOBA_WIDTH, D_MODEL)),
                  _const_spec((RET_V_WIDTH, D_MODEL)), _const_spec((D_MODEL, D_MODEL)),
                  _const_spec((1, D_MODEL)), _const_spec((D_MODEL, D_FF)), _const_spec((D_FF, D_MODEL)),
                  _const_spec((1, D_MODEL))],
        out_specs=row(D_MODEL),
        compiler_params=_params(1),
        name="merge_mlp",
    )(x, y_s5, y_moba, y_ret, gates, w["w_br_s5"], w["w_br_moba"], w["w_br_ret"], w["w_out"],
      w["norm2_g"], w["w_mlp_up"], w["w_mlp_down"], w["final_g"])


def _rope_tables(pos):
    half = MOBA_HEAD_DIM // 2
    freqs = ROPE_THETA ** (-jnp.arange(half, dtype=F32) / half)
    ang = pos.astype(F32)[:, None] * freqs[None, :]
    cos = jnp.cos(ang)
    sin = jnp.sin(ang)
    reps = LANES // MOBA_HEAD_DIM
    return (jnp.concatenate([cos, cos] * reps, axis=1), jnp.concatenate([-sin, sin] * reps, axis=1))


def kernel(x_prompt, x_sample, cache_k, cache_v, state_s5_re, state_s5_im, state_ret, page_table,
           norm1_g, w_in, s5_lambda_re, s5_lambda_im, s5_log_dt, s5_b_re, s5_b_im, s5_c_re, s5_c_im,
           s5_d, s5_w_glu, w_br_s5, w_br_moba, w_br_ret, w_out, norm2_g, w_mlp_up, w_mlp_down, final_g):
    bsz_p, seq_p, _ = x_prompt.shape
    bsz_s, seq_s, _ = x_sample.shape
    depth = w_in.shape[0]
    n_pages = page_table.shape[1]
    past_len = n_pages * PAGE_SIZE
    n_p = bsz_p * seq_p
    n_s = bsz_s * seq_s
    prompt_steps = ROW_TILE // SUBLANES
    assert seq_p % ROW_TILE == 0 and n_s % ROW_TILE == 0 and seq_p % RET_CHUNK == 0
    assert ROW_TILE == MOBA_BLOCK
    assert MOBA_BLOCK % PROMPT_Q_TILE == 0 and seq_p % PROMPT_Q_TILE == 0
    assert n_pages % (SAMPLE_BLOCKS_PER_STEP * MOBA_BLOCK // PAGE_SIZE) == 0
    assert seq_s == SUBLANES and bsz_s % SUBLANES == 0 and seq_s % RET_CHUNK != 0

    cos_p, sin_p = _rope_tables(jnp.tile(jnp.arange(seq_p, dtype=jnp.int32), bsz_p))
    cos_s, sin_s = _rope_tables(jnp.tile(past_len + jnp.arange(seq_s, dtype=jnp.int32), bsz_s))
    zeros_s5 = jnp.zeros((bsz_p, 1, S5_FLAT), F32)
    zeros_ret = jnp.zeros((bsz_p, RET_HEADS, RET_DK, RET_DV), F32)
    cache_kt = cache_k.transpose(0, 1, 3, 4, 2)
    cache_vt = cache_v.transpose(0, 1, 3, 4, 2)

    hp = x_prompt.reshape(n_p, D_MODEL)
    hs = x_sample.reshape(n_s, D_MODEL)
    outs = {name: [] for name in ("kp", "vp", "ks", "vs", "s5rp", "s5ip", "s5rs", "s5is", "retp", "rets")}
    for l in range(depth):
        last = l == depth - 1
        w = {"w_br_s5": w_br_s5[l].astype(BF16), "w_br_moba": w_br_moba[l].astype(BF16),
             "w_br_ret": w_br_ret[l].astype(BF16), "w_out": w_out[l].astype(BF16),
             "norm2_g": norm2_g[l].reshape(1, D_MODEL), "w_mlp_up": w_mlp_up[l].astype(BF16),
             "w_mlp_down": w_mlp_down[l].astype(BF16), "final_g": final_g.reshape(1, D_MODEL)}
        g1 = norm1_g[l].reshape(1, D_MODEL)
        w_in_l = w_in[l].astype(BF16)
        s5p = _s5_prepare(s5_lambda_re[l], s5_lambda_im[l], s5_log_dt[l], s5_b_re[l], s5_b_im[l],
                          s5_c_re[l], s5_c_im[l], s5_d[l], s5_w_glu[l], (prompt_steps, seq_s))

        n_blk = seq_p // MOBA_BLOCK
        (u, qm, km_t, vm_t, kb, vt, qr, kr, vr, gr, gates, ksum) = _inproj(hp, g1, w_in_l, cos_p, sin_p,
                                                                           seq_tiles=n_blk)
        seq = lambda t: t.reshape(bsz_p, seq_p, t.shape[-1])
        y_s5, s5r, s5i = _s5_mixer(seq(u), zeros_s5, zeros_s5, s5p, steps=prompt_steps, chained=True)
        y_moba = _moba_prompt(seq(qm), seq(kb), vt, ksum.reshape(bsz_p, n_blk, MOBA_WIDTH))
        y_ret, ret_s = _retention(seq(qr), seq(kr), seq(vr), seq(gr), zeros_ret, RET_CHUNK)
        hp = _merge(hp, y_s5.reshape(n_p, -1), y_moba.reshape(n_p, -1), y_ret.reshape(n_p, -1), gates, w, last)
        per_head = lambda t: t.reshape(bsz_p, MOBA_HEADS, MOBA_HEAD_DIM, seq_p).transpose(0, 3, 1, 2)
        outs["kp"].append(per_head(km_t))
        outs["vp"].append(per_head(vm_t))
        outs["s5rp"].append(s5r.reshape(bsz_p, S5_GROUPS, S5_STATE))
        outs["s5ip"].append(s5i.reshape(bsz_p, S5_GROUPS, S5_STATE))
        outs["retp"].append(ret_s)

        (u, qm, km, vm, kb, vb, qr, kr, vr, gr, gates, ksum) = _inproj(hs, g1, w_in_l, cos_s, sin_s)
        grp = lambda t: t.reshape(bsz_s // SUBLANES, SUBLANES * seq_s, t.shape[-1])
        st = lambda t: t.astype(F32).reshape(bsz_s // SUBLANES, SUBLANES, S5_FLAT)
        y_s5, s5r, s5i = _s5_mixer(grp(u), st(state_s5_re[l]), st(state_s5_im[l]), s5p,
                                   steps=seq_s, chained=False)
        seq = lambda t: t.reshape(bsz_s, seq_s, t.shape[-1])
        y_moba = _moba_sample(l, page_table, seq(qm), seq(km), seq(vm), cache_kt, cache_vt)
        y_ret, ret_s = _retention(seq(qr), seq(kr), seq(vr), seq(gr), state_ret[l], seq_s)
        hs = _merge(hs, y_s5.reshape(n_s, -1), y_moba.reshape(n_s, -1), y_ret.reshape(n_s, -1), gates, w, last)
        outs["ks"].append(km.reshape(bsz_s, seq_s, MOBA_HEADS, MOBA_HEAD_DIM))
        outs["vs"].append(vm.reshape(bsz_s, seq_s, MOBA_HEADS, MOBA_HEAD_DIM))
        outs["s5rs"].append(s5r.reshape(bsz_s, S5_GROUPS, S5_STATE))
        outs["s5is"].append(s5i.reshape(bsz_s, S5_GROUPS, S5_STATE))
        outs["rets"].append(ret_s)

    stack = lambda name: jnp.stack(outs[name])
    return (hp.reshape(x_prompt.shape), hs.reshape(x_sample.shape),
            stack("kp"), stack("vp"), stack("ks"), stack("vs"),
            stack("s5rp"), stack("s5ip"), stack("s5rs"), stack("s5is"), stack("retp"), stack("rets"))
```
